```python
import math
import jax, jax.numpy as jnp
from jax import lax
import numpy as np

D_MODEL = 2048
BATCH = 8
SEQ = 4096
DEPTH = 1
DEC_BATCH = 2
DEC_SEQ = 16384
PAST_LEN = 128

MLA_HEADS = 10
QK_NOPE = 128
QK_ROPE = 64
V_HEAD = 128
Q_LORA = 768
KV_LORA = 256
ROPE_THETA = 10000.0
Q_BLOCK = 128
DIL_HEADS = 12
DIL_HEAD_DIM = 64
DIL_PATTERNS = ((128, 1), (512, 4), (2048, 16))
BAND_BLOCK = 64
REL_BUCKETS = 32
REL_MAX_DIST = 1024
PEER_HEADS = 8
PEER_NKEYS = 128
PEER_EXPERTS = PEER_NKEYS * PEER_NKEYS
PEER_TOPK = 16
PEER_DKEY = 256
TOKEN_BLOCK = 128
EPS = 1e-6

DIL_WIDTH = DIL_HEADS * DIL_HEAD_DIM
MLA_OUT = MLA_HEADS * V_HEAD
MLA_QK = QK_NOPE + QK_ROPE
IN_SPLITS = (Q_LORA, KV_LORA, QK_ROPE, DIL_WIDTH, DIL_WIDTH, DIL_WIDTH)
IN_DIM = Q_LORA + KV_LORA + QK_ROPE + 3 * DIL_WIDTH

kernel_name = "hybrid_mla_dilated_peer_encoder"


def rms_norm(x, g):
    xf = x.astype(jnp.float32)
    y = xf * lax.rsqrt(jnp.mean(xf * xf, axis=-1, keepdims=True) + EPS)
    return (y * g.astype(jnp.float32)).astype(x.dtype)


def rotary(x, pos):
    half = QK_ROPE // 2
    inv = ROPE_THETA ** (-jnp.arange(half, dtype=jnp.float32) / half)
    ang = pos.astype(jnp.float32)[:, None] * inv[None, :]
    cos = jnp.cos(ang)[:, None, :]
    sin = jnp.sin(ang)[:, None, :]
    xf = x.astype(jnp.float32)
    x1, x2 = xf[..., :half], xf[..., half:]
    return jnp.concatenate([x1 * cos - x2 * sin, x1 * sin + x2 * cos], axis=-1).astype(x.dtype)


def t5_bucket(rel):
    nb = REL_BUCKETS // 2
    max_exact = nb // 2
    sign = (rel > 0).astype(jnp.int32) * nb
    n = jnp.abs(rel)
    large = max_exact + (jnp.log(jnp.maximum(n, 1).astype(jnp.float32) / max_exact)
                         / math.log(REL_MAX_DIST / max_exact) * (nb - max_exact)).astype(jnp.int32)
    large = jnp.minimum(large, nb - 1)
    return sign + jnp.where(n < max_exact, n, large)


def mla_attention(c_q, c_kv, k_rope, g_cq, w_uq, g_ckv, w_ukv, g_qn, g_kn):
    B, S, _ = c_q.shape
    pos = jnp.arange(S)
    q = (rms_norm(c_q, g_cq) @ w_uq).reshape(B, S, MLA_HEADS, MLA_QK)
    kv = (rms_norm(c_kv, g_ckv) @ w_ukv).reshape(B, S, MLA_HEADS, QK_NOPE + V_HEAD)
    k_nope, v = kv[..., :QK_NOPE], kv[..., QK_NOPE:]
    q = jnp.concatenate([q[..., :QK_NOPE], rotary(q[..., QK_NOPE:], pos)], axis=-1)
    k_r = rotary(k_rope[:, :, None, :], pos)
    k = jnp.concatenate([k_nope, jnp.broadcast_to(k_r, (B, S, MLA_HEADS, QK_ROPE))], axis=-1)
    q = rms_norm(q, g_qn)
    k = rms_norm(k, g_kn)
    scale = MLA_QK ** -0.5
    qb = q.reshape(B, S // Q_BLOCK, Q_BLOCK, MLA_HEADS, MLA_QK).transpose(1, 0, 2, 3, 4)

    def attend(q_blk):
        s = jnp.einsum('bqhd,bkhd->bhqk', q_blk, k, preferred_element_type=jnp.float32) * scale
        p = jax.nn.softmax(s, axis=-1)
        return jnp.einsum('bhqk,bkhd->bqhd', p.astype(v.dtype), v)

    o = lax.map(attend, qb)
    return o.transpose(1, 0, 2, 3, 4).reshape(B, S, MLA_OUT)


def dilated_branch(q, k, v, rel_bias, dil, half):
    B, S, H, Dh = q.shape
    L = S // dil
    BB = BAND_BLOCK
    nblk = -(-L // BB)
    Lp = nblk * BB
    nwin = 2 * half // BB + 1

    def to_sub(t):
        return t.reshape(B, L, dil, H, Dh).transpose(0, 2, 1, 3, 4)

    qs, ks, vs = to_sub(q), to_sub(k), to_sub(v)
    qs = jnp.pad(qs, ((0, 0), (0, 0), (0, Lp - L), (0, 0), (0, 0))).reshape(B, dil, nblk, BB, H, Dh)
    pad = ((0, 0), (0, 0), (half, Lp - L + half), (0, 0), (0, 0))
    kp, vp = jnp.pad(ks, pad), jnp.pad(vs, pad)
    kw = jnp.concatenate([kp[:, :, j * BB:j * BB + Lp].reshape(B, dil, nblk, BB, H, Dh) for j in range(nwin)], axis=3)
    vw = jnp.concatenate([vp[:, :, j * BB:j * BB + Lp].reshape(B, dil, nblk, BB, H, Dh) for j in range(nwin)], axis=3)
    KW = nwin * BB
    kk = jnp.arange(KW)
    qq = jnp.arange(BB)
    rel_sub = kk[None, :] - half - qq[:, None]
    band = jnp.abs(rel_sub) <= half
    key_idx = jnp.arange(nblk)[:, None] * BB - half + kk[None, :]
    inrange = (key_idx >= 0) & (key_idx < L)
    mask = band[None, :, :] & inrange[:, None, :]
    bias = rel_bias[t5_bucket(dil * rel_sub)].transpose(2, 0, 1)
    s = jnp.einsum('brnqhd,brnkhd->brnhqk', qs, kw, preferred_element_type=jnp.float32) * (Dh ** -0.5)
    s = s + bias.astype(jnp.float32)[None, None, None]
    s = jnp.where(mask[None, None, :, None], s, -1e30)
    m = jnp.max(s, axis=-1, keepdims=True)
    p = jnp.exp(s - m)
    den = jnp.sum(p, axis=-1)
    num = jnp.einsum('brnhqk,brnkhd->brnqhd', p, vw.astype(jnp.float32))

    def from_sub(t):
        t = t.reshape((B, dil, Lp) + t.shape[4:])[:, :, :L]
        t = jnp.swapaxes(t, 1, 2)
        return t.reshape((B, S) + t.shape[3:])

    num = from_sub(num)
    den = from_sub(jnp.swapaxes(den, 3, 4))
    mx = from_sub(jnp.swapaxes(m[..., 0], 3, 4))
    return num, den, mx


def dilated_attention(q, k, v, rel_bias, g_qn, g_kn):
    q = rms_norm(q, g_qn)
    k = rms_norm(k, g_kn)
    outs = [dilated_branch(q, k, v, rel_bias, dil, window // (2 * dil)) for window, dil in DIL_PATTERNS]
    m_all = outs[0][2]
    for _, _, m in outs[1:]:
        m_all = jnp.maximum(m_all, m)
    num_tot = 0.0
    den_tot = 0.0
    for num, den, m in outs:
        w = jnp.exp(m - m_all)
        num_tot = num_tot + w[..., None] * num
        den_tot = den_tot + w * den
    o = num_tot / den_tot[..., None]
    B, S = q.shape[:2]
    return o.reshape(B, S, DIL_WIDTH).astype(v.dtype)


def peer_ffn(h, w_pq, sub_keys, expert_u, expert_v):
    B, S, D = h.shape
    hb = h.reshape(B * S // TOKEN_BLOCK, TOKEN_BLOCK, D)

    def block(xb):
        q = (xb @ w_pq).reshape(TOKEN_BLOCK, PEER_HEADS, 2, PEER_DKEY // 2)
        s = jnp.einsum('thcd,hcnd->thcn', q, sub_keys, preferred_element_type=jnp.float32)
        v1, i1 = lax.top_k(s[:, :, 0], PEER_TOPK)
        v2, i2 = lax.top_k(s[:, :, 1], PEER_TOPK)
        cand = (v1[..., :, None] + v2[..., None, :]).reshape(TOKEN_BLOCK, PEER_HEADS, PEER_TOPK * PEER_TOPK)
        sc, ci = lax.top_k(cand, PEER_TOPK)
        e1 = jnp.take_along_axis(i1, ci // PEER_TOPK, axis=-1)
        e2 = jnp.take_along_axis(i2, ci % PEER_TOPK, axis=-1)
        idx = e1 * PEER_NKEYS + e2
        g = jax.nn.softmax(sc, axis=-1)
        u = expert_u[idx]
        a = jax.nn.gelu(jnp.einsum('thkd,td->thk', u, xb, preferred_element_type=jnp.float32))
        coef = (g * a).astype(xb.dtype)
        return jnp.einsum('thk,thkd->td', coef, expert_v[idx])

    return lax.map(block, hb).reshape(B, S, D)


def encoder_trunk(x, g_attn, w_in, g_cq, w_uq, g_ckv, w_ukv, g_mla_qn, g_mla_kn,
                  g_dil_qn, g_dil_kn, rel_bias, w_out, g_ffn, w_pq, peer_subkeys, peer_u, peer_v):
    B, S, _ = x.shape
    for l in range(DEPTH):
        h = rms_norm(x, g_attn[l])
        z = h @ w_in[l]
        offs = np.cumsum((0,) + IN_SPLITS)
        c_q, c_kv, k_rope, qd, kd, vd = [z[..., int(offs[i]):int(offs[i + 1])] for i in range(len(IN_SPLITS))]
        o_mla = mla_attention(c_q, c_kv, k_rope, g_cq[l], w_uq[l], g_ckv[l], w_ukv[l], g_mla_qn[l], g_mla_kn[l])
        shp = (B, S, DIL_HEADS, DIL_HEAD_DIM)
        o_dil = dilated_attention(qd.reshape(shp), kd.reshape(shp), vd.reshape(shp), rel_bias, g_dil_qn[l], g_dil_kn[l])
        x = x + jnp.concatenate([o_mla, o_dil], axis=-1) @ w_out[l]
        x = x + peer_ffn(rms_norm(x, g_ffn[l]), w_pq[l], peer_subkeys[l], peer_u[l], peer_v[l])
    return x


def setup_inputs(seed: int = 0) -> dict:
    key = jax.random.key(seed)
    ks = jax.random.split(key, 20)
    f = jnp.float32

    def nrm(k, shape, scale):
        return jax.random.normal(k, shape, f) * scale

    def gain(k, shape):
        return 1.0 + 0.05 * jax.random.normal(k, shape, f)

    L = DEPTH
    return {
        "x_prompt": nrm(ks[0], (BATCH, SEQ, D_MODEL), 1.0),
        "x_sample": nrm(ks[1], (DEC_BATCH, DEC_SEQ, D_MODEL), 1.0),
        "g_attn": gain(ks[2], (L, D_MODEL)),
        "w_in": nrm(ks[3], (L, D_MODEL, IN_DIM), D_MODEL ** -0.5),
        "g_cq": gain(ks[4], (L, Q_LORA)),
        "w_uq": nrm(ks[5], (L, Q_LORA, MLA_HEADS * MLA_QK), Q_LORA ** -0.5),
        "g_ckv": gain(ks[6], (L, KV_LORA)),
        "w_ukv": nrm(ks[7], (L, KV_LORA, MLA_HEADS * (QK_NOPE + V_HEAD)), KV_LORA ** -0.5),
        "g_mla_qn": gain(ks[8], (L, MLA_QK)),
        "g_mla_kn": gain(ks[9], (L, MLA_QK)),
        "g_dil_qn": gain(ks[10], (L, DIL_HEAD_DIM)),
        "g_dil_kn": gain(ks[11], (L, DIL_HEAD_DIM)),
        "rel_bias": nrm(ks[12], (REL_BUCKETS, DIL_HEADS), 0.5),
        "w_out": nrm(ks[13], (L, MLA_OUT + DIL_WIDTH, D_MODEL), (MLA_OUT + DIL_WIDTH) ** -0.5),
        "g_ffn": gain(ks[14], (L, D_MODEL)),
        "w_pq": nrm(ks[15], (L, D_MODEL, PEER_HEADS * PEER_DKEY), D_MODEL ** -0.5),
        "peer_subkeys": nrm(ks[16], (L, PEER_HEADS, 2, PEER_NKEYS, PEER_DKEY // 2), (PEER_DKEY // 2) ** -0.5),
        "peer_u": nrm(ks[17], (L, PEER_EXPERTS, D_MODEL), D_MODEL ** -0.5),
        "peer_v": nrm(ks[18], (L, PEER_EXPERTS, D_MODEL), PEER_HEADS ** -0.5),
    }


def reference(x_prompt, x_sample, g_attn, w_in, g_cq, w_uq, g_ckv, w_ukv, g_mla_qn, g_mla_kn,
              g_dil_qn, g_dil_kn, rel_bias, w_out, g_ffn, w_pq, peer_subkeys, peer_u, peer_v):
    y_prompt = encoder_trunk(x_prompt, g_attn, w_in, g_cq, w_uq, g_ckv, w_ukv, g_mla_qn, g_mla_kn,
                             g_dil_qn, g_dil_kn, rel_bias, w_out, g_ffn, w_pq, peer_subkeys, peer_u, peer_v)
    y_sample = encoder_trunk(x_sample, g_attn, w_in, g_cq, w_uq, g_ckv, w_ukv, g_mla_qn, g_mla_kn,
                             g_dil_qn, g_dil_kn, rel_bias, w_out, g_ffn, w_pq, peer_subkeys, peer_u, peer_v)
    return (y_prompt, y_sample)
```

```python
import functools
import math

import jax
import jax.numpy as jnp
from jax import lax
from jax.experimental import pallas as pl
from jax.experimental.pallas import tpu as pltpu

F32 = jnp.float32
BF16 = jnp.bfloat16
LOG2E = 1.4426950408889634
NEG = -1e30
EPS = 1e-6

D_MODEL = 2048
MLA_HEADS = 10
QK_NOPE = 128
QK_ROPE = 64
V_HEAD = 128
Q_LORA = 768
KV_LORA = 256
ROPE_THETA = 10000.0
MLA_QK = QK_NOPE + QK_ROPE
MLA_OUT = MLA_HEADS * V_HEAD
MLA_PAD = 256
DIL_HEADS = 12
DIL_HEAD_DIM = 64
DIL_WIDTH = DIL_HEADS * DIL_HEAD_DIM
DIL_PATTERNS = ((128, 1), (512, 4), (2048, 16))
DIL_REACH = 1024
REL_BUCKETS = 32
REL_MAX_DIST = 1024
PEER_HEADS = 8
PEER_NKEYS = 128
PEER_TOPK = 16
PEER_DKEY = 256
LANES = 128

VMEM_LIMIT = 56 * 1024 * 1024


def _cparams(sem):
    return pltpu.CompilerParams(dimension_semantics=sem, vmem_limit_bytes=VMEM_LIMIT)


def _dot(a, b):
    return jnp.dot(a, b, preferred_element_type=F32)


def _dot_nt(a, b):
    return lax.dot_general(a, b, (((1,), (1,)), ((), ())), preferred_element_type=F32)


def _dot_tn(a, b):
    return lax.dot_general(a, b, (((0,), (0,)), ((), ())), preferred_element_type=F32)


def _dot_hilo(a, m):
    hi = a.astype(BF16)
    lo = (a - hi.astype(F32)).astype(BF16)
    return _dot(hi, m) + _dot(lo, m)


def _rms(x, g):
    return x * lax.rsqrt(jnp.mean(x * x, axis=-1, keepdims=True) + EPS) * g


IN_COLS = Q_LORA + KV_LORA + 3 * DIL_WIDTH + 2 * QK_ROPE
ZC = Q_LORA + KV_LORA


def _in_proj_kernel(x_ref, g_ref, w_ref, seg_ref, segt_ref, gq_ref, gk_ref,
                    zc_ref, zr_ref, qd_ref, kd_ref, vd_ref):
    h = _rms(x_ref[...], g_ref[...])
    z = _dot(h.astype(BF16), w_ref[...])
    zc_ref[...] = z[:, :ZC]
    zr_ref[...] = z[:, ZC + 3 * DIL_WIDTH:]

    def head_norm(t, g):
        ms = _dot_hilo(t * t, seg_ref[...]) * (1.0 / DIL_HEAD_DIM)
        r = _dot_hilo(lax.rsqrt(ms + EPS), segt_ref[...])
        return (t * r * g).astype(BF16)

    qd_ref[...] = head_norm(z[:, ZC:ZC + DIL_WIDTH], gq_ref[...])
    kd_ref[...] = head_norm(z[:, ZC + DIL_WIDTH:ZC + 2 * DIL_WIDTH], gk_ref[...])
    vd_ref[...] = z[:, ZC + 2 * DIL_WIDTH:ZC + 3 * DIL_WIDTH].astype(BF16)


def _in_proj(x, g_attn, w1, seg, segt, gq, gk, tm=256):
    T = x.shape[0]
    row = lambda i: (i, 0)
    fix = lambda i: (0, 0)
    return pl.pallas_call(
        _in_proj_kernel,
        grid=(T // tm,),
        in_specs=[pl.BlockSpec((tm, D_MODEL), row),
                  pl.BlockSpec((1, D_MODEL), fix),
                  pl.BlockSpec((D_MODEL, IN_COLS), fix),
                  pl.BlockSpec((DIL_WIDTH, LANES), fix),
                  pl.BlockSpec((LANES, DIL_WIDTH), fix),
                  pl.BlockSpec((1, DIL_WIDTH), fix),
                  pl.BlockSpec((1, DIL_WIDTH), fix)],
        out_specs=[pl.BlockSpec((tm, ZC), row),
                   pl.BlockSpec((tm, LANES), row),
                   pl.BlockSpec((tm, DIL_WIDTH), row),
                   pl.BlockSpec((tm, DIL_WIDTH), row),
                   pl.BlockSpec((tm, DIL_WIDTH), row)],
        out_shape=[jax.ShapeDtypeStruct((T, ZC), F32),
                   jax.ShapeDtypeStruct((T, LANES), F32),
                   jax.ShapeDtypeStruct((T, DIL_WIDTH), BF16),
                   jax.ShapeDtypeStruct((T, DIL_WIDTH), BF16),
                   jax.ShapeDtypeStruct((T, DIL_WIDTH), BF16)],
        compiler_params=_cparams(("parallel",)),
        name="in_proj",
    )(x, g_attn, w1, seg, segt, gq, gk)


def _mla_prep_kernel(zc_ref, zr_ref, cs_ref, gcq_ref, gckv_ref, wq_ref, wkv_ref, gq_ref, gk_ref,
                     q_ref, k_ref, v_ref):
    zc = zc_ref[...]
    cqn = _rms(zc[:, :Q_LORA], gcq_ref[...]).astype(BF16)
    ckvn = _rms(zc[:, Q_LORA:], gckv_ref[...]).astype(BF16)
    cs = cs_ref[...]
    lower = lax.broadcasted_iota(jnp.int32, cs.shape, 1) < QK_ROPE

    def rotate(r):
        p = r * cs
        return jnp.where(lower, p + pltpu.roll(p, QK_ROPE, 1), 0.0)

    gq = gq_ref[...]
    gk = gk_ref[...]
    kr = rotate(zr_ref[...])
    kr_ss = jnp.sum(kr * kr, axis=-1, keepdims=True)
    for h in range(MLA_HEADS):
        c0 = h * MLA_PAD
        qh = _dot(cqn, wq_ref[:, c0:c0 + MLA_PAD])
        nope = qh[:, :QK_NOPE]
        rope = rotate(qh[:, QK_NOPE:])
        ss = jnp.sum(nope * nope + rope * rope, axis=-1, keepdims=True)
        r = lax.rsqrt(ss * (1.0 / MLA_QK) + EPS)
        q_ref[:, c0:c0 + QK_NOPE] = (nope * r * gq[:, :QK_NOPE]).astype(BF16)
        q_ref[:, c0 + QK_NOPE:c0 + MLA_PAD] = (rope * r * gq[:, QK_NOPE:]).astype(BF16)
        kv = _dot(ckvn, wkv_ref[:, c0:c0 + MLA_PAD])
        kn = kv[:, :QK_NOPE]
        rk = lax.rsqrt((jnp.sum(kn * kn, axis=-1, keepdims=True) + kr_ss) * (1.0 / MLA_QK) + EPS)
        k_ref[:, c0:c0 + QK_NOPE] = (kn * rk * gk[:, :QK_NOPE]).astype(BF16)
        k_ref[:, c0 + QK_NOPE:c0 + MLA_PAD] = (kr * rk * gk[:, QK_NOPE:]).astype(BF16)
        v_ref[:, h * V_HEAD:(h + 1) * V_HEAD] = kv[:, QK_NOPE:].astype(BF16)


def _mla_prep(zc, zr, cs, gcq, gckv, wq, wkv, gq, gk, tm=256):
    T = zc.shape[0]
    row = lambda i: (i, 0)
    fix = lambda i: (0, 0)
    W = MLA_HEADS * MLA_PAD
    return pl.pallas_call(
        _mla_prep_kernel,
        grid=(T // tm,),
        in_specs=[pl.BlockSpec((tm, ZC), row),
                  pl.BlockSpec((tm, LANES), row),
                  pl.BlockSpec((tm, LANES), row),
                  pl.BlockSpec((1, Q_LORA), fix),
                  pl.BlockSpec((1, KV_LORA), fix),
                  pl.BlockSpec((Q_LORA, W), fix),
                  pl.BlockSpec((KV_LORA, W), fix),
                  pl.BlockSpec((1, MLA_PAD), fix),
                  pl.BlockSpec((1, MLA_PAD), fix)],
        out_specs=[pl.BlockSpec((tm, W), row),
                   pl.BlockSpec((tm, W), row),
                   pl.BlockSpec((tm, MLA_OUT), row)],
        out_shape=[jax.ShapeDtypeStruct((T, W), BF16),
                   jax.ShapeDtypeStruct((T, W), BF16),
                   jax.ShapeDtypeStruct((T, MLA_OUT), BF16)],
        compiler_params=_cparams(("parallel",)),
        name="mla_prep",
    )(zc, zr, cs, gcq, gckv, wq, wkv, gq, gk)


def _mla_attn_kernel(*refs, tk, nk, aliased):
    q_ref, k_ref, v_ref = refs[:3]
    o_ref = refs[-1]
    q = q_ref[...]
    tq = q.shape[0]
    ones = jnp.ones((tk, V_HEAD), BF16)

    def body(j, carry):
        m, acc = carry
        start = pl.multiple_of(j * tk, tk)
        kj = k_ref[pl.ds(start, tk), :]
        vj = v_ref[pl.ds(start, tk), :]
        s = _dot_nt(q, kj)
        m_new = jnp.maximum(m, jnp.max(s, axis=-1, keepdims=True))
        p = jnp.exp2(s - m_new)
        alpha = jnp.exp2(m - m_new)
        vext = jnp.concatenate([vj, ones], axis=1)
        acc = acc * alpha + _dot(p.astype(BF16), vext)
        return m_new, acc

    m0 = jnp.full((tq, 1), NEG, F32)
    acc0 = jnp.zeros((tq, 2 * V_HEAD), F32)
    _, acc = lax.fori_loop(0, nk, body, (m0, acc0))
    o_ref[...] = (acc[:, :V_HEAD] / acc[:, V_HEAD:]).astype(o_ref.dtype)


def _mla_attn(q, k, v, o_prev, B, S, tok_off, tq=256, tk=512):
    T = q.shape[0]
    tk = min(tk, S)
    nq = S // tq
    qoff = tok_off // tq
    soff = tok_off // S
    in_specs = [pl.BlockSpec((tq, MLA_PAD), lambda b, h, i: (qoff + b * nq + i, h)),
                pl.BlockSpec((S, MLA_PAD), lambda b, h, i: (soff + b, h)),
                pl.BlockSpec((S, V_HEAD), lambda b, h, i: (soff + b, h))]
    args = [q, k, v]
    aliases = {}
    if o_prev is not None:
        in_specs.append(pl.BlockSpec(memory_space=pl.ANY))
        args.append(o_prev)
        aliases = {3: 0}
    return pl.pallas_call(
        functools.partial(_mla_attn_kernel, tk=tk, nk=S // tk, aliased=o_prev is not None),
        grid=(B, MLA_HEADS, nq),
        in_specs=in_specs,
        out_specs=pl.BlockSpec((tq, V_HEAD), lambda b, h, i: (qoff + b * nq + i, h)),
        out_shape=jax.ShapeDtypeStruct((T, MLA_OUT), BF16),
        input_output_aliases=aliases,
        compiler_params=_cparams(("parallel", "parallel", "arbitrary")),
        name="mla_attn",
    )(*args)


DIL_TILE = 256
DIL_NOFF = 2 * (DIL_REACH // DIL_TILE) + 1


def _dil_attn_kernel(*refs, nk):
    q_ref, k_ref, v_ref, bias_ref = refs[:4]
    o_ref, m_s, l_s, acc_s = refs[-4:]
    i = pl.program_id(2)
    q = q_ref[...]
    t = DIL_TILE
    first = lax.broadcasted_iota(jnp.int32, q.shape, 1) < DIL_HEAD_DIM
    zero = jnp.zeros_like(q)
    qh = (jnp.where(first, q, zero), jnp.where(first, zero, q))
    m_s[...] = jnp.full(m_s.shape, NEG, F32)
    l_s[...] = jnp.zeros(l_s.shape, F32)
    acc_s[...] = jnp.zeros(acc_s.shape, F32)
    half = DIL_NOFF // 2
    for d in range(DIL_NOFF):
        j = i + (d - half)

        @pl.when(jnp.logical_and(j >= 0, j < nk))
        def _():
            start = pl.multiple_of(j * t, t)
            kj = k_ref[pl.ds(start, t), :]
            vj = v_ref[pl.ds(start, t), :]
            for hh in range(2):
                s = _dot_nt(qh[hh], kj) + bias_ref[hh, d]
                m_old = m_s[hh]
                m_new = jnp.maximum(m_old, jnp.max(s, axis=-1, keepdims=True))
                p = jnp.exp2(s - m_new)
                alpha = jnp.exp2(m_old - m_new)
                l_s[hh] = alpha * l_s[hh] + jnp.sum(p, axis=-1, keepdims=True)
                acc_s[hh] = alpha * acc_s[hh] + _dot(p.astype(BF16), vj)
                m_s[hh] = m_new

    o0 = acc_s[0] / l_s[0]
    o1 = acc_s[1] / l_s[1]
    o_ref[...] = jnp.where(first, o0, o1).astype(o_ref.dtype)


def _dil_attn(q, k, v, bias, o_prev, B, S, tok_off):
    T = q.shape[0]
    t = DIL_TILE
    nq = S // t
    qoff = tok_off // t
    soff = tok_off // S
    in_specs = [pl.BlockSpec((t, LANES), lambda b, h, i: (qoff + b * nq + i, h)),
                pl.BlockSpec((S, LANES), lambda b, h, i: (soff + b, h)),
                pl.BlockSpec((S, LANES), lambda b, h, i: (soff + b, h)),
                pl.BlockSpec((2, DIL_NOFF, t, t), lambda b, h, i: (h, 0, 0, 0))]
    args = [q, k, v, bias]
    aliases = {}
    if o_prev is not None:
        in_specs.append(pl.BlockSpec(memory_space=pl.ANY))
        args.append(o_prev)
        aliases = {4: 0}
    return pl.pallas_call(
        functools.partial(_dil_attn_kernel, nk=nq),
        grid=(B, DIL_HEADS // 2, nq),
        in_specs=in_specs,
        out_specs=pl.BlockSpec((t, LANES), lambda b, h, i: (qoff + b * nq + i, h)),
        out_shape=jax.ShapeDtypeStruct((T, DIL_WIDTH), BF16),
        scratch_shapes=[pltpu.VMEM((2, t, 1), F32),
                        pltpu.VMEM((2, t, 1), F32),
                        pltpu.VMEM((2, t, LANES), F32)],
        input_output_aliases=aliases,
        compiler_params=_cparams(("parallel", "parallel", "arbitrary")),
        name="dil_attn",
    )(*args)


def _t5_bucket(rel):
    nb = REL_BUCKETS // 2
    max_exact = nb // 2
    sign = (rel > 0).astype(jnp.int32) * nb
    n = jnp.abs(rel)
    large = max_exact + (jnp.log(jnp.maximum(n, 1).astype(F32) / max_exact)
                         / math.log(REL_MAX_DIST / max_exact) * (nb - max_exact)).astype(jnp.int32)
    large = jnp.minimum(large, nb - 1)
    return sign + jnp.where(n < max_exact, n, large)


def _dil_bias_tiles(rel_bias):
    t = DIL_TILE
    span = DIL_REACH + t - 1
    rel = jnp.arange(-span, span + 1)
    n = jnp.abs(rel)
    count = jnp.zeros(rel.shape, jnp.int32)
    for window, dil in DIL_PATTERNS:
        count = count + ((rel % dil == 0) & (n <= window // 2)).astype(jnp.int32)
    table = rel_bias.astype(F32)[_t5_bucket(rel)].T
    table = (table + jnp.log(jnp.maximum(count, 1).astype(F32))[None]) * LOG2E
    table = jnp.where((count > 0)[None], table, NEG)
    d = jnp.arange(DIL_NOFF)[:, None, None] - DIL_NOFF // 2
    idx = d * t + jnp.arange(t)[None, None, :] - jnp.arange(t)[None, :, None] + span
    return table[:, idx]


def _out_proj_kernel(x_ref, om_ref, od_ref, wa_ref, wb_ref, g_ref, x1_ref, h2_ref):
    x1 = x_ref[...] + _dot(om_ref[...], wa_ref[...]) + _dot(od_ref[...], wb_ref[...])
    x1_ref[...] = x1
    h2_ref[...] = _rms(x1, g_ref[...]).astype(BF16)


def _out_proj(x, o_mla, o_dil, wa, wb, g_ffn, tm=256):
    T = x.shape[0]
    row = lambda i: (i, 0)
    fix = lambda i: (0, 0)
    return pl.pallas_call(
        _out_proj_kernel,
        grid=(T // tm,),
        in_specs=[pl.BlockSpec((tm, D_MODEL), row),
                  pl.BlockSpec((tm, MLA_OUT), row),
                  pl.BlockSpec((tm, DIL_WIDTH), row),
                  pl.BlockSpec((MLA_OUT, D_MODEL), fix),
                  pl.BlockSpec((DIL_WIDTH, D_MODEL), fix),
                  pl.BlockSpec((1, D_MODEL), fix)],
        out_specs=[pl.BlockSpec((tm, D_MODEL), row),
                   pl.BlockSpec((tm, D_MODEL), row)],
        out_shape=[jax.ShapeDtypeStruct((T, D_MODEL), F32),
                   jax.ShapeDtypeStruct((T, D_MODEL), BF16)],
        compiler_params=_cparams(("parallel",)),
        name="out_proj",
    )(x, o_mla, o_dil, wa, wb, g_ffn)


PEER_HC = PEER_HEADS * 2
PEER_HALF = PEER_DKEY // 2


def _peer_query_kernel(h2_ref, w_ref, keys_ref, s_ref):
    pq = _dot(h2_ref[...], w_ref[...]).astype(BF16)
    for hc in range(PEER_HC):
        s_ref[hc] = _dot_nt(keys_ref[hc], pq[:, hc * PEER_HALF:(hc + 1) * PEER_HALF])


def _peer_query(h2, w_pq, keys, tm=256):
    T = h2.shape[0]
    return pl.pallas_call(
        _peer_query_kernel,
        grid=(T // tm,),
        in_specs=[pl.BlockSpec((tm, D_MODEL), lambda i: (i, 0)),
                  pl.BlockSpec((D_MODEL, PEER_HEADS * PEER_DKEY), lambda i: (0, 0)),
                  pl.BlockSpec((PEER_HC, PEER_NKEYS, PEER_HALF), lambda i: (0, 0, 0))],
        out_specs=pl.BlockSpec((PEER_HC, PEER_NKEYS, tm), lambda i: (0, 0, i)),
        out_shape=jax.ShapeDtypeStruct((PEER_HC, PEER_NKEYS, T), F32),
        compiler_params=_cparams(("parallel",)),
        name="peer_query",
    )(h2, w_pq, keys)


def _peer_topk_kernel(s_ref, thr_ref, e1_ref, e2_ref, v1_s, v2_s):
    def top_values(x, out_s):
        def body(k, x):
            mx = jnp.max(x, axis=0, keepdims=True)
            out_s[pl.ds(k, 1), :] = mx
            return jnp.where(x >= mx, -jnp.inf, x)
        lax.fori_loop(0, PEER_TOPK, body, x)

    for h in range(PEER_HEADS):
        a = s_ref[h, 0]
        b = s_ref[h, 1]
        top_values(a, v1_s)
        top_values(b, v2_s)
        v1 = v1_s[...]
        v2 = v2_s[...]
        cand = jnp.concatenate([v1[k:k + 1] + v2 for k in range(PEER_TOPK)], axis=0)

        def body(k, c):
            x, _ = c
            mx = jnp.max(x, axis=0, keepdims=True)
            return jnp.where(x >= mx, -jnp.inf, x), mx

        _, tau = lax.fori_loop(0, PEER_TOPK, body, (cand, jnp.zeros_like(v1[0:1])))
        cmax = v1[0:1] + v2[0:1]
        z = jnp.sum(jnp.where(cand >= tau, jnp.exp(cand - cmax), 0.0), axis=0, keepdims=True)
        thr_ref[h] = tau - a
        e1_ref[h] = jnp.exp(a - v1[0:1])
        e2_ref[h] = jnp.exp(b - v2[0:1]) / z


def _peer_topk(s4, tl=256):
    T = s4.shape[-1]
    out = jax.ShapeDtypeStruct((PEER_HEADS, PEER_NKEYS, T), F32)
    ospec = pl.BlockSpec((PEER_HEADS, PEER_NKEYS, tl), lambda i: (0, 0, i))
    return pl.pallas_call(
        _peer_topk_kernel,
        grid=(T // tl,),
        in_specs=[pl.BlockSpec((PEER_HEADS, 2, PEER_NKEYS, tl), lambda i: (0, 0, 0, i))],
        out_specs=[ospec, ospec, ospec],
        out_shape=[out, out, out],
        scratch_shapes=[pltpu.VMEM((PEER_TOPK, tl), F32), pltpu.VMEM((PEER_TOPK, tl), F32)],
        compiler_params=_cparams(("parallel",)),
        name="peer_topk",
    )(s4)


PEER_TB = 512
PEER_NE1 = 4


def _peer_expert_kernel(h2_ref, u_ref, v_ref, thr_ref, e1_ref, b_ref, e2_ref, x1_ref, y_ref, coef_s):
    j = pl.program_id(1)

    @pl.when(j == 0)
    def _():
        y_ref[...] = x1_ref[...]

    at = _dot_nt(u_ref[...], h2_ref[...])
    n = PEER_NKEYS
    for l in range(PEER_NE1):
        first = j * PEER_NE1 + l
        g = jnp.zeros((n, at.shape[1]), F32)
        for h in range(PEER_HEADS):
            thr = thr_ref[h, pl.ds(first, 1), :]
            w1 = e1_ref[h, pl.ds(first, 1), :]
            g = g + jnp.where(b_ref[h] >= thr, e2_ref[h] * w1, 0.0)
        coef_s[l * n:(l + 1) * n, :] = (g * jax.nn.gelu(at[l * n:(l + 1) * n])).astype(BF16)
    y_ref[...] += _dot_tn(coef_s[...], v_ref[...])


def _peer_expert(h2, u, v, thr, e1, s4, e2, x1):
    T = h2.shape[0]
    tb = PEER_TB
    ec = PEER_NE1 * PEER_NKEYS
    tok = pl.BlockSpec((PEER_HEADS, PEER_NKEYS, tb), lambda i, j: (0, 0, i))
    return pl.pallas_call(
        _peer_expert_kernel,
        grid=(T // tb, u.shape[0] // ec),
        in_specs=[pl.BlockSpec((tb, D_MODEL), lambda i, j: (i, 0)),
                  pl.BlockSpec((ec, D_MODEL), lambda i, j: (j, 0)),
                  pl.BlockSpec((ec, D_MODEL), lambda i, j: (j, 0)),
                  tok, tok,
                  pl.BlockSpec((PEER_HEADS, None, PEER_NKEYS, tb), lambda i, j: (0, 1, 0, i)),
                  tok,
                  pl.BlockSpec((tb, D_MODEL), lambda i, j: (i, 0))],
        out_specs=pl.BlockSpec((tb, D_MODEL), lambda i, j: (i, 0)),
        out_shape=jax.ShapeDtypeStruct((T, D_MODEL), F32),
        scratch_shapes=[pltpu.VMEM((ec, tb), BF16)],
        compiler_params=_cparams(("parallel", "arbitrary")),
        name="peer_expert",
    )(h2, u, v, thr, e1, s4, e2, x1)


def _rope_table(segments):
    half = QK_ROPE // 2
    inv = ROPE_THETA ** (-jnp.arange(half, dtype=F32) / half)
    pos = jnp.concatenate([jnp.tile(jnp.arange(S), B) for B, S, _ in segments])
    ang = pos.astype(F32)[:, None] * inv[None, :]
    cos, sin = jnp.cos(ang), jnp.sin(ang)
    return jnp.concatenate([cos, cos, sin, sin], axis=-1)


def _rot_half_cols(w):
    half = QK_ROPE // 2
    return jnp.concatenate([-w[..., half:], w[..., :half]], axis=-1)


def kernel(x_prompt, x_sample, g_attn, w_in, g_cq, w_uq, g_ckv, w_ukv, g_mla_qn, g_mla_kn,
           g_dil_qn, g_dil_kn, rel_bias, w_out, g_ffn, w_pq, peer_subkeys, peer_u, peer_v):
    segments = []
    off = 0
    for xs in (x_prompt, x_sample):
        B, S, _ = xs.shape
        segments.append((B, S, off))
        off += B * S
    T = off
    x = jnp.concatenate([x_prompt.reshape(-1, D_MODEL), x_sample.reshape(-1, D_MODEL)], axis=0)
    l = 0

    o = [0, Q_LORA, Q_LORA + KV_LORA, Q_LORA + KV_LORA + QK_ROPE]
    wi = w_in[l]
    w_cq, w_ckv, w_kr, w_dil = wi[:, o[0]:o[1]], wi[:, o[1]:o[2]], wi[:, o[2]:o[3]], wi[:, o[3]:]
    w1 = jnp.concatenate([w_cq, w_ckv, w_dil, w_kr, _rot_half_cols(w_kr)], axis=1).astype(BF16)
    head_of = jnp.arange(DIL_WIDTH) // DIL_HEAD_DIM
    seg = (head_of[:, None] == jnp.arange(LANES)[None, :]).astype(BF16)
    segt = seg.T
    dil_scale = DIL_HEAD_DIM ** -0.5 * LOG2E
    gq_d = (jnp.tile(g_dil_qn[l], DIL_HEADS) * dil_scale)[None]
    gk_d = jnp.tile(g_dil_kn[l], DIL_HEADS)[None]

    wq3 = w_uq[l].reshape(Q_LORA, MLA_HEADS, MLA_QK)
    wq = jnp.concatenate([wq3, _rot_half_cols(wq3[..., QK_NOPE:])], axis=-1)
    wq = wq.reshape(Q_LORA, MLA_HEADS * MLA_PAD).astype(BF16)
    wkv = w_ukv[l].astype(BF16)
    pad = jnp.zeros((MLA_PAD - MLA_QK,), F32)
    mla_scale = MLA_QK ** -0.5 * LOG2E
    gq_m = jnp.concatenate([g_mla_qn[l] * mla_scale, pad])[None]
    gk_m = jnp.concatenate([g_mla_kn[l], pad])[None]
    cs = _rope_table(segments)
    bias = _dil_bias_tiles(rel_bias)

    wo = w_out[l].astype(BF16)
    keys = peer_subkeys[l].reshape(PEER_HC, PEER_NKEYS, PEER_HALF).astype(BF16)
    u = peer_u[l].astype(BF16)
    v = peer_v[l].astype(BF16)

    zc, zr, qd, kd, vd = _in_proj(x, g_attn[l][None], w1, seg, segt, gq_d, gk_d)
    q, k, vv = _mla_prep(zc, zr, cs, g_cq[l][None], g_ckv[l][None], wq, wkv, gq_m, gk_m)
    o_mla = None
    o_dil = None
    for B, S, toff in segments:
        o_mla = _mla_attn(q, k, vv, o_mla, B, S, toff)
        o_dil = _dil_attn(qd, kd, vd, bias, o_dil, B, S, toff)
    x1, h2 = _out_proj(x, o_mla, o_dil, wo[:MLA_OUT], wo[MLA_OUT:], g_ffn[l][None])
    s = _peer_query(h2, w_pq[l].astype(BF16), keys)
    s4 = s.reshape(PEER_HEADS, 2, PEER_NKEYS, T)
    thr, e1, e2 = _peer_topk(s4)
    y = _peer_expert(h2, u, v, thr, e1, s4, e2, x1)
    n_prompt = x_prompt.shape[0] * x_prompt.shape[1]
    return (y[:n_prompt].reshape(x_prompt.shape), y[n_prompt:].reshape(x_sample.shape))
```

```python
import functools
import math

import jax
import jax.numpy as jnp
from jax import lax
from jax.experimental import pallas as pl
from jax.experimental.pallas import tpu as pltpu

F32 = jnp.float32
BF16 = jnp.bfloat16
LOG2E = 1.4426950408889634
NEG = -1e30
EPS = 1e-6

D_MODEL = 2048
MLA_HEADS = 10
QK_NOPE = 128
QK_ROPE = 64
V_HEAD = 128
Q_LORA = 768
KV_LORA = 256
ROPE_THETA = 10000.0
MLA_QK = QK_NOPE + QK_ROPE
MLA_OUT = MLA_HEADS * V_HEAD
MLA_PAD = 256
DIL_HEADS = 12
DIL_HEAD_DIM = 64
DIL_WIDTH = DIL_HEADS * DIL_HEAD_DIM
DIL_PATTERNS = ((128, 1), (512, 4), (2048, 16))
DIL_REACH = 1024
REL_BUCKETS = 32
REL_MAX_DIST = 1024
PEER_HEADS = 8
PEER_NKEYS = 128
PEER_TOPK = 16
PEER_DKEY = 256
LANES = 128
SUBLANES = 8

VMEM_LIMIT = 56 * 1024 * 1024


def _cparams(sem):
    return pltpu.CompilerParams(dimension_semantics=sem, vmem_limit_bytes=VMEM_LIMIT)


def _dot(a, b):
    return jnp.dot(a, b, preferred_element_type=F32)


def _dot_nt(a, b):
    return lax.dot_general(a, b, (((1,), (1,)), ((), ())), preferred_element_type=F32)


def _dot_tn(a, b):
    return lax.dot_general(a, b, (((0,), (0,)), ((), ())), preferred_element_type=F32)


def _dot_hilo(a, m):
    hi = a.astype(BF16)
    lo = (a - hi.astype(F32)).astype(BF16)
    return _dot(hi, m) + _dot(lo, m)


def _rms(x, g):
    return x * lax.rsqrt(jnp.mean(x * x, axis=-1, keepdims=True) + EPS) * g


IN_COLS = Q_LORA + KV_LORA + 3 * DIL_WIDTH + 2 * QK_ROPE
ZC = Q_LORA + KV_LORA


def _in_proj_kernel(x_ref, g_ref, w_ref, seg_ref, segt_ref, gq_ref, gk_ref,
                    zc_ref, zr_ref, qd_ref, kd_ref, vd_ref):
    h = _rms(x_ref[...], g_ref[...])
    z = _dot(h.astype(BF16), w_ref[...])
    zc_ref[...] = z[:, :ZC]
    zr_ref[...] = z[:, ZC + 3 * DIL_WIDTH:]

    def head_norm(t, g):
        ms = _dot_hilo(t * t, seg_ref[...]) * (1.0 / DIL_HEAD_DIM)
        r = _dot_hilo(lax.rsqrt(ms + EPS), segt_ref[...])
        return (t * r * g).astype(BF16)

    qd_ref[...] = head_norm(z[:, ZC:ZC + DIL_WIDTH], gq_ref[...])
    kd_ref[...] = head_norm(z[:, ZC + DIL_WIDTH:ZC + 2 * DIL_WIDTH], gk_ref[...])
    vd_ref[...] = z[:, ZC + 2 * DIL_WIDTH:ZC + 3 * DIL_WIDTH].astype(BF16)


def _in_proj(x, g_attn, w1, seg, segt, gq, gk, tm=256):
    T = x.shape[0]
    row = lambda i: (i, 0)
    fix = lambda i: (0, 0)
    return pl.pallas_call(
        _in_proj_kernel,
        grid=(T // tm,),
        in_specs=[pl.BlockSpec((tm, D_MODEL), row),
                  pl.BlockSpec((1, D_MODEL), fix),
                  pl.BlockSpec((D_MODEL, IN_COLS), fix),
                  pl.BlockSpec((DIL_WIDTH, LANES), fix),
                  pl.BlockSpec((LANES, DIL_WIDTH), fix),
                  pl.BlockSpec((1, DIL_WIDTH), fix),
                  pl.BlockSpec((1, DIL_WIDTH), fix)],
        out_specs=[pl.BlockSpec((tm, ZC), row),
                   pl.BlockSpec((tm, LANES), row),
                   pl.BlockSpec((tm, DIL_WIDTH), row),
                   pl.BlockSpec((tm, DIL_WIDTH), row),
                   pl.BlockSpec((tm, DIL_WIDTH), row)],
        out_shape=[jax.ShapeDtypeStruct((T, ZC), F32),
                   jax.ShapeDtypeStruct((T, LANES), F32),
                   jax.ShapeDtypeStruct((T, DIL_WIDTH), BF16),
                   jax.ShapeDtypeStruct((T, DIL_WIDTH), BF16),
                   jax.ShapeDtypeStruct((T, DIL_WIDTH), BF16)],
        compiler_params=_cparams(("parallel",)),
        name="in_proj",
    )(x, g_attn, w1, seg, segt, gq, gk)


def _mla_prep_kernel(zc_ref, zr_ref, cs_ref, gcq_ref, gckv_ref, wq_ref, wkv_ref, gq_ref, gk_ref,
                     q_ref, k_ref, v_ref):
    zc = zc_ref[...]
    cqn = _rms(zc[:, :Q_LORA], gcq_ref[...]).astype(BF16)
    ckvn = _rms(zc[:, Q_LORA:], gckv_ref[...]).astype(BF16)
    cs = cs_ref[...]
    lower = lax.broadcasted_iota(jnp.int32, cs.shape, 1) < QK_ROPE

    def rotate(r):
        p = r * cs
        return jnp.where(lower, p + pltpu.roll(p, QK_ROPE, 1), 0.0)

    gq = gq_ref[...]
    gk = gk_ref[...]
    kr = rotate(zr_ref[...])
    kr_ss = jnp.sum(kr * kr, axis=-1, keepdims=True)
    for h in range(MLA_HEADS):
        c0 = h * MLA_PAD
        qh = _dot(cqn, wq_ref[:, c0:c0 + MLA_PAD])
        nope = qh[:, :QK_NOPE]
        rope = rotate(qh[:, QK_NOPE:])
        ss = jnp.sum(nope * nope + rope * rope, axis=-1, keepdims=True)
        r = lax.rsqrt(ss * (1.0 / MLA_QK) + EPS)
        q_ref[:, c0:c0 + QK_NOPE] = (nope * r * gq[:, :QK_NOPE]).astype(BF16)
        q_ref[:, c0 + QK_NOPE:c0 + MLA_PAD] = (rope * r * gq[:, QK_NOPE:]).astype(BF16)
        kv = _dot(ckvn, wkv_ref[:, c0:c0 + MLA_PAD])
        kn = kv[:, :QK_NOPE]
        rk = lax.rsqrt((jnp.sum(kn * kn, axis=-1, keepdims=True) + kr_ss) * (1.0 / MLA_QK) + EPS)
        k_ref[:, c0:c0 + QK_NOPE] = (kn * rk * gk[:, :QK_NOPE]).astype(BF16)
        k_ref[:, c0 + QK_NOPE:c0 + MLA_PAD] = (kr * rk * gk[:, QK_NOPE:]).astype(BF16)
        v_ref[:, h * V_HEAD:(h + 1) * V_HEAD] = kv[:, QK_NOPE:].astype(BF16)


def _mla_prep(zc, zr, cs, gcq, gckv, wq, wkv, gq, gk, tm=256):
    T = zc.shape[0]
    row = lambda i: (i, 0)
    fix = lambda i: (0, 0)
    W = MLA_HEADS * MLA_PAD
    return pl.pallas_call(
        _mla_prep_kernel,
        grid=(T // tm,),
        in_specs=[pl.BlockSpec((tm, ZC), row),
                  pl.BlockSpec((tm, LANES), row),
                  pl.BlockSpec((tm, LANES), row),
                  pl.BlockSpec((1, Q_LORA), fix),
                  pl.BlockSpec((1, KV_LORA), fix),
                  pl.BlockSpec((Q_LORA, W), fix),
                  pl.BlockSpec((KV_LORA, W), fix),
                  pl.BlockSpec((1, MLA_PAD), fix),
                  pl.BlockSpec((1, MLA_PAD), fix)],
        out_specs=[pl.BlockSpec((tm, W), row),
                   pl.BlockSpec((tm, W), row),
                   pl.BlockSpec((tm, MLA_OUT), row)],
        out_shape=[jax.ShapeDtypeStruct((T, W), BF16),
                   jax.ShapeDtypeStruct((T, W), BF16),
                   jax.ShapeDtypeStruct((T, MLA_OUT), BF16)],
        compiler_params=_cparams(("parallel",)),
        name="mla_prep",
    )(zc, zr, cs, gcq, gckv, wq, wkv, gq, gk)


DIRECT_SOFTMAX_MAX = 120.0
ATTN_SUB = 256


def _mla_attn_kernel(*refs, tk, nk, online):
    q_ref, k_ref, v_ref = refs[:3]
    o_ref = refs[-1]
    q = q_ref[...]
    tq = q.shape[0]
    sub = tk if online else ATTN_SUB
    ones = jnp.ones((sub, V_HEAD), BF16)

    def chunk(start):
        start = pl.multiple_of(start, sub)
        s = _dot_nt(q, k_ref[pl.ds(start, sub), :])
        vext = jnp.concatenate([v_ref[pl.ds(start, sub), :], ones], axis=1)
        return s, vext

    if online:
        def body(j, carry):
            m, acc = carry
            s, vext = chunk(j * tk)
            m_new = jnp.maximum(m, jnp.max(s, axis=-1, keepdims=True))
            p = jnp.exp2(s - m_new)
            alpha = jnp.exp2(m - m_new)
            return m_new, acc * alpha + _dot(p.astype(BF16), vext)

        init = (jnp.full((tq, 1), NEG, F32), jnp.zeros((tq, 2 * V_HEAD), F32))
        _, acc = lax.fori_loop(0, nk, body, init)
    else:
        def body(j, acc):
            for c in range(tk // sub):
                s, vext = chunk(j * tk + c * sub)
                acc = acc + _dot(jnp.exp2(s).astype(BF16), vext)
            return acc

        acc = lax.fori_loop(0, nk, body, jnp.zeros((tq, 2 * V_HEAD), F32))
    o_ref[...] = (acc[:, :V_HEAD] / acc[:, V_HEAD:]).astype(o_ref.dtype)


def _mla_attn(q, k, v, o_prev, B, S, tok_off, online):
    T = q.shape[0]
    tq, tk = (256, 512) if online else (512, 2048)
    tk = min(tk, S)
    nq = S // tq
    qoff = tok_off // tq
    soff = tok_off // S
    in_specs = [pl.BlockSpec((tq, MLA_PAD), lambda b, h, i: (qoff + b * nq + i, h)),
                pl.BlockSpec((S, MLA_PAD), lambda b, h, i: (soff + b, h)),
                pl.BlockSpec((S, V_HEAD), lambda b, h, i: (soff + b, h))]
    args = [q, k, v]
    aliases = {}
    if o_prev is not None:
        in_specs.append(pl.BlockSpec(memory_space=pl.ANY))
        args.append(o_prev)
        aliases = {3: 0}
    return pl.pallas_call(
        functools.partial(_mla_attn_kernel, tk=tk, nk=S // tk, online=online),
        grid=(B, MLA_HEADS, nq),
        in_specs=in_specs,
        out_specs=pl.BlockSpec((tq, V_HEAD), lambda b, h, i: (qoff + b * nq + i, h)),
        out_shape=jax.ShapeDtypeStruct((T, MLA_OUT), BF16),
        input_output_aliases=aliases,
        compiler_params=_cparams(("parallel", "parallel", "arbitrary")),
        name="mla_attn",
    )(*args)


DIL_TILE = 256
DIL_NOFF = 2 * (DIL_REACH // DIL_TILE) + 1


def _dil_attn_kernel(*refs, nk, online):
    q_ref, k_ref, v_ref, bias_ref = refs[:4]
    i = pl.program_id(2)
    q = q_ref[...]
    t = DIL_TILE
    first = lax.broadcasted_iota(jnp.int32, q.shape, 1) < DIL_HEAD_DIM
    zero = jnp.zeros_like(q)
    half = DIL_NOFF // 2

    if online:
        o_ref, m_s, l_s, acc_s = refs[-4:]
        qh = (jnp.where(first, q, zero), jnp.where(first, zero, q))
        m_s[...] = jnp.full(m_s.shape, NEG, F32)
        l_s[...] = jnp.zeros(l_s.shape, F32)
        acc_s[...] = jnp.zeros(acc_s.shape, F32)
        for d in range(DIL_NOFF):
            j = i + (d - half)

            @pl.when(jnp.logical_and(j >= 0, j < nk))
            def _():
                start = pl.multiple_of(j * t, t)
                kj = k_ref[pl.ds(start, t), :]
                vj = v_ref[pl.ds(start, t), :]
                for hh in range(2):
                    s = _dot_nt(qh[hh], kj) + bias_ref[hh, :, d * t:(d + 1) * t]
                    m_old = m_s[hh]
                    m_new = jnp.maximum(m_old, jnp.max(s, axis=-1, keepdims=True))
                    p = jnp.exp2(s - m_new)
                    alpha = jnp.exp2(m_old - m_new)
                    l_s[hh] = alpha * l_s[hh] + jnp.sum(p, axis=-1, keepdims=True)
                    acc_s[hh] = alpha * acc_s[hh] + _dot(p.astype(BF16), vj)
                    m_s[hh] = m_new

        o0 = acc_s[0] / l_s[0]
        o1 = acc_s[1] / l_s[1]
    else:
        o_ref = refs[-1]
        qq = jnp.concatenate([jnp.where(first, q, zero), jnp.where(first, zero, q)], axis=0)
        ones = jnp.ones((t, LANES), BF16)
        acc = jnp.zeros((2 * t, 2 * LANES), F32)
        for d in range(DIL_NOFF):
            j = i + (d - half)
            valid = jnp.logical_and(j >= 0, j < nk)
            start = pl.multiple_of(jnp.clip(j, 0, nk - 1) * t, t)
            bias = jnp.where(valid, bias_ref[:, :, d * t:(d + 1) * t].reshape(2 * t, t), NEG)
            s = _dot_nt(qq, k_ref[pl.ds(start, t), :]) + bias
            vext = jnp.concatenate([v_ref[pl.ds(start, t), :], ones], axis=1)
            acc = acc + _dot(jnp.exp2(s).astype(BF16), vext)
        o0 = acc[:t, :LANES] / acc[:t, LANES:]
        o1 = acc[t:, :LANES] / acc[t:, LANES:]
    o_ref[...] = jnp.where(first, o0, o1).astype(o_ref.dtype)


def _dil_attn(q, k, v, bias, o_prev, B, S, tok_off, online):
    T = q.shape[0]
    t = DIL_TILE
    nq = S // t
    qoff = tok_off // t
    soff = tok_off // S
    in_specs = [pl.BlockSpec((t, LANES), lambda b, h, i: (qoff + b * nq + i, h)),
                pl.BlockSpec((S, LANES), lambda b, h, i: (soff + b, h)),
                pl.BlockSpec((S, LANES), lambda b, h, i: (soff + b, h)),
                pl.BlockSpec((2, t, DIL_NOFF * t), lambda b, h, i: (h, 0, 0))]
    args = [q, k, v, bias]
    aliases = {}
    if o_prev is not None:
        in_specs.append(pl.BlockSpec(memory_space=pl.ANY))
        args.append(o_prev)
        aliases = {4: 0}
    scratch = [pltpu.VMEM((2, t, 1), F32), pltpu.VMEM((2, t, 1), F32), pltpu.VMEM((2, t, LANES), F32)]
    return pl.pallas_call(
        functools.partial(_dil_attn_kernel, nk=nq, online=online),
        grid=(B, DIL_HEADS // 2, nq),
        in_specs=in_specs,
        out_specs=pl.BlockSpec((t, LANES), lambda b, h, i: (qoff + b * nq + i, h)),
        out_shape=jax.ShapeDtypeStruct((T, DIL_WIDTH), BF16),
        scratch_shapes=scratch if online else [],
        input_output_aliases=aliases,
        compiler_params=_cparams(("parallel", "parallel", "arbitrary")),
        name="dil_attn",
    )(*args)


def _t5_bucket(rel):
    nb = REL_BUCKETS // 2
    max_exact = nb // 2
    sign = (rel > 0).astype(jnp.int32) * nb
    n = jnp.abs(rel)
    large = max_exact + (jnp.log(jnp.maximum(n, 1).astype(F32) / max_exact)
                         / math.log(REL_MAX_DIST / max_exact) * (nb - max_exact)).astype(jnp.int32)
    large = jnp.minimum(large, nb - 1)
    return sign + jnp.where(n < max_exact, n, large)


def _dil_bias_table(rel_bias):
    t = DIL_TILE
    span = DIL_REACH + t - 1
    rel = jnp.arange(-span, span + 1)
    n = jnp.abs(rel)
    count = jnp.zeros(rel.shape, jnp.int32)
    for window, dil in DIL_PATTERNS:
        count = count + ((rel % dil == 0) & (n <= window // 2)).astype(jnp.int32)
    table = rel_bias.astype(F32)[_t5_bucket(rel)].T
    table = (table + jnp.log(jnp.maximum(count, 1).astype(F32))[None]) * LOG2E
    table = jnp.where((count > 0)[None], table, NEG)
    width = rel.shape[0]
    padded = jnp.pad(table, ((0, 0), (0, 1)))
    shifts = jnp.tile(padded, (1, t))[:, :t * width].reshape(DIL_HEADS, t, width)
    return shifts[:, :, t - 1:t - 1 + DIL_NOFF * t]


def _out_proj_kernel(x_ref, om_ref, od_ref, wa_ref, wb_ref, g_ref, x1_ref, h2_ref):
    x1 = x_ref[...] + _dot(om_ref[...], wa_ref[...]) + _dot(od_ref[...], wb_ref[...])
    x1_ref[...] = x1
    h2_ref[...] = _rms(x1, g_ref[...]).astype(BF16)


def _out_proj(x, o_mla, o_dil, wa, wb, g_ffn, tm=256):
    T = x.shape[0]
    row = lambda i: (i, 0)
    fix = lambda i: (0, 0)
    return pl.pallas_call(
        _out_proj_kernel,
        grid=(T // tm,),
        in_specs=[pl.BlockSpec((tm, D_MODEL), row),
                  pl.BlockSpec((tm, MLA_OUT), row),
                  pl.BlockSpec((tm, DIL_WIDTH), row),
                  pl.BlockSpec((MLA_OUT, D_MODEL), fix),
                  pl.BlockSpec((DIL_WIDTH, D_MODEL), fix),
                  pl.BlockSpec((1, D_MODEL), fix)],
        out_specs=[pl.BlockSpec((tm, D_MODEL), row),
                   pl.BlockSpec((tm, D_MODEL), row)],
        out_shape=[jax.ShapeDtypeStruct((T, D_MODEL), F32),
                   jax.ShapeDtypeStruct((T, D_MODEL), BF16)],
        compiler_params=_cparams(("parallel",)),
        name="out_proj",
    )(x, o_mla, o_dil, wa, wb, g_ffn)


PEER_HC = PEER_HEADS * 2
PEER_HALF = PEER_DKEY // 2


def _peer_query_kernel(h2_ref, w_ref, keys_ref, s_ref):
    pq = _dot(h2_ref[...], w_ref[...]).astype(BF16)
    for hc in range(PEER_HC):
        s_ref[hc] = _dot_nt(keys_ref[hc], pq[:, hc * PEER_HALF:(hc + 1) * PEER_HALF])


def _peer_query(h2, w_pq, keys, tm=256):
    T = h2.shape[0]
    return pl.pallas_call(
        _peer_query_kernel,
        grid=(T // tm,),
        in_specs=[pl.BlockSpec((tm, D_MODEL), lambda i: (i, 0)),
                  pl.BlockSpec((D_MODEL, PEER_HEADS * PEER_DKEY), lambda i: (0, 0)),
                  pl.BlockSpec((PEER_HC, PEER_NKEYS, PEER_HALF), lambda i: (0, 0, 0))],
        out_specs=pl.BlockSpec((PEER_HC, PEER_NKEYS, tm), lambda i: (0, 0, i)),
        out_shape=jax.ShapeDtypeStruct((PEER_HC, PEER_NKEYS, T), F32),
        compiler_params=_cparams(("parallel",)),
        name="peer_query",
    )(h2, w_pq, keys)


def _next_max(x, prev):
    return jnp.max(jnp.where(x < prev, x, -jnp.inf), axis=0, keepdims=True)


def _peer_topk_kernel(s_ref, thr_ref, e1_ref, e2_ref, top_s):
    K = PEER_TOPK
    inf_row = jnp.full((1, s_ref.shape[-1]), jnp.inf, F32)

    def top_body(k, prevs):
        mxs = [_next_max(s_ref[hc // 2, hc % 2], prevs[hc]) for hc in range(PEER_HC)]
        for hc, mx in enumerate(mxs):
            top_s[hc, pl.ds(k, 1), :] = mx
        return mxs

    lax.fori_loop(0, K, top_body, [inf_row] * PEER_HC)

    cands = []
    for h in range(PEER_HEADS):
        v1 = top_s[2 * h]
        v2 = top_s[2 * h + 1]
        rows = [v1 + v2[0:1]]
        rows += [v1[:SUBLANES] + v2[k2:k2 + 1] for k2 in range(1, SUBLANES)]
        rows += [v1[0:1] + v2[SUBLANES:]]
        cands.append(jnp.concatenate(rows, axis=0))

    def tau_body(k, prevs):
        return [_next_max(c, p) for c, p in zip(cands, prevs)]

    taus = lax.fori_loop(0, K, tau_body, [inf_row] * PEER_HEADS)

    for h in range(PEER_HEADS):
        a = s_ref[h, 0]
        b = s_ref[h, 1]
        amax = top_s[2 * h, 0:1]
        bmax = top_s[2 * h + 1, 0:1]
        cand = cands[h]
        z = jnp.sum(jnp.where(cand >= taus[h], jnp.exp(cand - (amax + bmax)), 0.0), axis=0, keepdims=True)
        thr_ref[h] = taus[h] - a
        e1_ref[h] = jnp.exp(a - amax)
        e2_ref[h] = jnp.exp(b - bmax) / z


def _peer_topk(s4, tl=128):
    T = s4.shape[-1]
    out = jax.ShapeDtypeStruct((PEER_HEADS, PEER_NKEYS, T), F32)
    ospec = pl.BlockSpec((PEER_HEADS, PEER_NKEYS, tl), lambda i: (0, 0, i))
    return pl.pallas_call(
        _peer_topk_kernel,
        grid=(T // tl,),
        in_specs=[pl.BlockSpec((PEER_HEADS, 2, PEER_NKEYS, tl), lambda i: (0, 0, 0, i))],
        out_specs=[ospec, ospec, ospec],
        out_shape=[out, out, out],
        scratch_shapes=[pltpu.VMEM((PEER_HC, PEER_TOPK, tl), F32)],
        compiler_params=_cparams(("parallel",)),
        name="peer_topk",
    )(s4)


PEER_TB = 512
PEER_NE1 = 8
PEER_CHAIN = 2


def _peer_expert_kernel(h2_ref, u_ref, v_ref, thr_ref, e1_ref, b_ref, e2_ref, x1_ref, y_ref, g_s):
    j = pl.program_id(1)

    @pl.when(j == 0)
    def _():
        y_ref[...] = x1_ref[...]

    h2 = h2_ref[...]
    n = PEER_NKEYS
    rows = PEER_CHAIN * n
    acc = None
    for c in range(PEER_NE1 // PEER_CHAIN):
        for l in range(PEER_CHAIN):
            first = j * PEER_NE1 + c * PEER_CHAIN + l
            g = jnp.zeros((n, h2.shape[0]), F32)
            for h in range(PEER_HEADS):
                thr = thr_ref[h, pl.ds(first, 1), :]
                w1 = e1_ref[h, pl.ds(first, 1), :]
                g = g + jnp.where(b_ref[h] >= thr, e2_ref[h] * w1, 0.0)
            g_s[c % 2, l * n:(l + 1) * n, :] = g
        at = _dot_nt(u_ref[c * rows:(c + 1) * rows, :], h2)
        coef = (g_s[c % 2] * jax.nn.gelu(at)).astype(BF16)
        part = _dot_tn(coef, v_ref[c * rows:(c + 1) * rows, :])
        acc = part if acc is None else acc + part
    y_ref[...] += acc


def _peer_expert(h2, u, v, thr, e1, s4, e2, x1):
    T = h2.shape[0]
    tb = PEER_TB
    ec = PEER_NE1 * PEER_NKEYS
    once = pl.Buffered(1)
    tok = pl.BlockSpec((PEER_HEADS, PEER_NKEYS, tb), lambda i, j: (0, 0, i), pipeline_mode=once)
    return pl.pallas_call(
        _peer_expert_kernel,
        grid=(T // tb, u.shape[0] // ec),
        in_specs=[pl.BlockSpec((tb, D_MODEL), lambda i, j: (i, 0), pipeline_mode=once),
                  pl.BlockSpec((ec, D_MODEL), lambda i, j: (j, 0)),
                  pl.BlockSpec((ec, D_MODEL), lambda i, j: (j, 0)),
                  tok, tok,
                  pl.BlockSpec((PEER_HEADS, None, PEER_NKEYS, tb), lambda i, j: (0, 1, 0, i),
                               pipeline_mode=once),
                  tok,
                  pl.BlockSpec((tb, D_MODEL), lambda i, j: (i, 0), pipeline_mode=once)],
        out_specs=pl.BlockSpec((tb, D_MODEL), lambda i, j: (i, 0)),
        out_shape=jax.ShapeDtypeStruct((T, D_MODEL), F32),
        scratch_shapes=[pltpu.VMEM((2, PEER_CHAIN * PEER_NKEYS, tb), F32)],
        compiler_params=_cparams(("parallel", "arbitrary")),
        name="peer_expert",
    )(h2, u, v, thr, e1, s4, e2, x1)


def _rope_table(segments):
    half = QK_ROPE // 2
    inv = ROPE_THETA ** (-jnp.arange(half, dtype=F32) / half)
    pos = jnp.concatenate([jnp.tile(jnp.arange(S), B) for B, S, _ in segments])
    ang = pos.astype(F32)[:, None] * inv[None, :]
    cos, sin = jnp.cos(ang), jnp.sin(ang)
    return jnp.concatenate([cos, cos, sin, sin], axis=-1)


def _rot_half_cols(w):
    half = QK_ROPE // 2
    return jnp.concatenate([-w[..., half:], w[..., :half]], axis=-1)


def _col_norm_max(w):
    return jnp.sqrt(jnp.max(jnp.sum(jnp.square(w.astype(F32)), axis=0)))


def _direct_softmax_ok(score_bound, n_keys, value_bound):
    return score_bound + jnp.log2(n_keys * jnp.maximum(value_bound, 1.0)) <= DIRECT_SOFTMAX_MAX


def kernel(x_prompt, x_sample, g_attn, w_in, g_cq, w_uq, g_ckv, w_ukv, g_mla_qn, g_mla_kn,
           g_dil_qn, g_dil_kn, rel_bias, w_out, g_ffn, w_pq, peer_subkeys, peer_u, peer_v):
    segments = []
    off = 0
    for xs in (x_prompt, x_sample):
        B, S, _ = xs.shape
        segments.append((B, S, off))
        off += B * S
    T = off
    x = jnp.concatenate([x_prompt.reshape(-1, D_MODEL), x_sample.reshape(-1, D_MODEL)], axis=0)
    l = 0

    o = [0, Q_LORA, Q_LORA + KV_LORA, Q_LORA + KV_LORA + QK_ROPE]
    wi = w_in[l]
    w_cq, w_ckv, w_kr, w_dil = wi[:, o[0]:o[1]], wi[:, o[1]:o[2]], wi[:, o[2]:o[3]], wi[:, o[3]:]
    w1 = jnp.concatenate([w_cq, w_ckv, w_dil, w_kr, _rot_half_cols(w_kr)], axis=1).astype(BF16)
    head_of = jnp.arange(DIL_WIDTH) // DIL_HEAD_DIM
    seg = (head_of[:, None] == jnp.arange(LANES)[None, :]).astype(BF16)
    segt = seg.T
    dil_scale = DIL_HEAD_DIM ** -0.5 * LOG2E
    gq_d = (jnp.tile(g_dil_qn[l], DIL_HEADS) * dil_scale)[None]
    gk_d = jnp.tile(g_dil_kn[l], DIL_HEADS)[None]

    wq3 = w_uq[l].reshape(Q_LORA, MLA_HEADS, MLA_QK)
    wq = jnp.concatenate([wq3, _rot_half_cols(wq3[..., QK_NOPE:])], axis=-1)
    wq = wq.reshape(Q_LORA, MLA_HEADS * MLA_PAD).astype(BF16)
    wkv = w_ukv[l].astype(BF16)
    pad = jnp.zeros((MLA_PAD - MLA_QK,), F32)
    mla_scale = MLA_QK ** -0.5 * LOG2E
    gq_m = jnp.concatenate([g_mla_qn[l] * mla_scale, pad])[None]
    gk_m = jnp.concatenate([g_mla_kn[l], pad])[None]
    cs = _rope_table(segments)
    bias = _dil_bias_table(rel_bias)

    wo = w_out[l].astype(BF16)
    keys = peer_subkeys[l].reshape(PEER_HC, PEER_NKEYS, PEER_HALF).astype(BF16)
    u = peer_u[l].astype(BF16)
    v = peer_v[l].astype(BF16)

    amax = lambda a: jnp.max(jnp.abs(a))
    s_max = max(S for _, S, _ in segments)
    mla_ok = _direct_softmax_ok(
        MLA_QK * amax(g_mla_qn[l]) * amax(g_mla_kn[l]) * mla_scale, s_max,
        KV_LORA ** 0.5 * amax(g_ckv[l]) * _col_norm_max(w_ukv[l]))
    dil_ok = _direct_softmax_ok(
        DIL_HEAD_DIM * amax(g_dil_qn[l]) * amax(g_dil_kn[l]) * dil_scale + amax(rel_bias) * LOG2E + 2.0,
        DIL_NOFF * DIL_TILE,
        D_MODEL ** 0.5 * amax(g_attn[l]) * _col_norm_max(w_dil[:, 2 * DIL_WIDTH:]))

    zc, zr, qd, kd, vd = _in_proj(x, g_attn[l][None], w1, seg, segt, gq_d, gk_d)
    q, k, vv = _mla_prep(zc, zr, cs, g_cq[l][None], g_ckv[l][None], wq, wkv, gq_m, gk_m)

    def attend(online):
        o_mla = None
        o_dil = None
        for B, S, toff in segments:
            o_mla = _mla_attn(q, k, vv, o_mla, B, S, toff, online)
            o_dil = _dil_attn(qd, kd, vd, bias, o_dil, B, S, toff, online)
        return o_mla, o_dil

    o_mla, o_dil = lax.cond(jnp.logical_and(mla_ok, dil_ok), lambda: attend(False), lambda: attend(True))
    x1, h2 = _out_proj(x, o_mla, o_dil, wo[:MLA_OUT], wo[MLA_OUT:], g_ffn[l][None])
    s = _peer_query(h2, w_pq[l].astype(BF16), keys)
    s4 = s.reshape(PEER_HEADS, 2, PEER_NKEYS, T)
    thr, e1, e2 = _peer_topk(s4)
    y = _peer_expert(h2, u, v, thr, e1, s4, e2, x1)
    n_prompt = x_prompt.shape[0] * x_prompt.shape[1]
    return (y[:n_prompt].reshape(x_prompt.shape), y[n_prompt:].reshape(x_sample.shape))
```

```python
import functools
import math

import jax
import jax.numpy as jnp
from jax import lax
from jax.experimental import pallas as pl
from jax.experimental.pallas import tpu as pltpu

F32 = jnp.float32
BF16 = jnp.bfloat16
LOG2E = 1.4426950408889634
NEG = -1e30
EPS = 1e-6

D_MODEL = 2048
MLA_HEADS = 10
QK_NOPE = 128
QK_ROPE = 64
V_HEAD = 128
Q_LORA = 768
KV_LORA = 256
ROPE_THETA = 10000.0
MLA_QK = QK_NOPE + QK_ROPE
MLA_OUT = MLA_HEADS * V_HEAD
MLA_PAD = 256
DIL_HEADS = 12
DIL_HEAD_DIM = 64
DIL_WIDTH = DIL_HEADS * DIL_HEAD_DIM
DIL_PATTERNS = ((128, 1), (512, 4), (2048, 16))
DIL_REACH = 1024
REL_BUCKETS = 32
REL_MAX_DIST = 1024
PEER_HEADS = 8
PEER_NKEYS = 128
PEER_TOPK = 16
PEER_DKEY = 256
LANES = 128
SUBLANES = 8

VMEM_LIMIT = 56 * 1024 * 1024


def _cparams(sem):
    return pltpu.CompilerParams(dimension_semantics=sem, vmem_limit_bytes=VMEM_LIMIT)


def _dot(a, b):
    return jnp.dot(a, b, preferred_element_type=F32)


def _dot_nt(a, b):
    return lax.dot_general(a, b, (((1,), (1,)), ((), ())), preferred_element_type=F32)


def _dot_tn(a, b):
    return lax.dot_general(a, b, (((0,), (0,)), ((), ())), preferred_element_type=F32)


def _dot_hilo(a, m):
    hi = a.astype(BF16)
    lo = (a - hi.astype(F32)).astype(BF16)
    return _dot(hi, m) + _dot(lo, m)


def _rms(x, g):
    return x * lax.rsqrt(jnp.mean(x * x, axis=-1, keepdims=True) + EPS) * g


IN_COLS = Q_LORA + KV_LORA + 3 * DIL_WIDTH + 2 * QK_ROPE
ZC = Q_LORA + KV_LORA


def _in_proj_kernel(x_ref, g_ref, w_ref, seg_ref, segt_ref, gq_ref, gk_ref,
                    zc_ref, zr_ref, qd_ref, kd_ref, vd_ref):
    h = _rms(x_ref[...], g_ref[...])
    z = _dot(h.astype(BF16), w_ref[...])
    zc_ref[...] = z[:, :ZC]
    zr_ref[...] = z[:, ZC + 3 * DIL_WIDTH:]

    def head_norm(t, g):
        ms = _dot_hilo(t * t, seg_ref[...]) * (1.0 / DIL_HEAD_DIM)
        r = _dot_hilo(lax.rsqrt(ms + EPS), segt_ref[...])
        return (t * r * g).astype(BF16)

    qd_ref[...] = head_norm(z[:, ZC:ZC + DIL_WIDTH], gq_ref[...])
    kd_ref[...] = head_norm(z[:, ZC + DIL_WIDTH:ZC + 2 * DIL_WIDTH], gk_ref[...])
    vd_ref[...] = z[:, ZC + 2 * DIL_WIDTH:ZC + 3 * DIL_WIDTH].astype(BF16)


def _in_proj(x, g_attn, w1, seg, segt, gq, gk, tm=256):
    T = x.shape[0]
    row = lambda i: (i, 0)
    fix = lambda i: (0, 0)
    return pl.pallas_call(
        _in_proj_kernel,
        grid=(T // tm,),
        in_specs=[pl.BlockSpec((tm, D_MODEL), row),
                  pl.BlockSpec((1, D_MODEL), fix),
                  pl.BlockSpec((D_MODEL, IN_COLS), fix),
                  pl.BlockSpec((DIL_WIDTH, LANES), fix),
                  pl.BlockSpec((LANES, DIL_WIDTH), fix),
                  pl.BlockSpec((1, DIL_WIDTH), fix),
                  pl.BlockSpec((1, DIL_WIDTH), fix)],
        out_specs=[pl.BlockSpec((tm, ZC), row),
                   pl.BlockSpec((tm, LANES), row),
                   pl.BlockSpec((tm, DIL_WIDTH), row),
                   pl.BlockSpec((tm, DIL_WIDTH), row),
                   pl.BlockSpec((tm, DIL_WIDTH), row)],
        out_shape=[jax.ShapeDtypeStruct((T, ZC), F32),
                   jax.ShapeDtypeStruct((T, LANES), F32),
                   jax.ShapeDtypeStruct((T, DIL_WIDTH), BF16),
                   jax.ShapeDtypeStruct((T, DIL_WIDTH), BF16),
                   jax.ShapeDtypeStruct((T, DIL_WIDTH), BF16)],
        compiler_params=_cparams(("parallel",)),
        name="in_proj",
    )(x, g_attn, w1, seg, segt, gq, gk)


def _mla_prep_kernel(zc_ref, zr_ref, cs_ref, gcq_ref, gckv_ref, wq_ref, wkv_ref, gq_ref, gk_ref,
                     q_ref, k_ref, v_ref):
    zc = zc_ref[...]
    cqn = _rms(zc[:, :Q_LORA], gcq_ref[...]).astype(BF16)
    ckvn = _rms(zc[:, Q_LORA:], gckv_ref[...]).astype(BF16)
    cs = cs_ref[...]
    lower = lax.broadcasted_iota(jnp.int32, cs.shape, 1) < QK_ROPE

    def rotate(r):
        p = r * cs
        return jnp.where(lower, p + pltpu.roll(p, QK_ROPE, 1), 0.0)

    gq = gq_ref[...]
    gk = gk_ref[...]
    kr = rotate(zr_ref[...])
    kr_ss = jnp.sum(kr * kr, axis=-1, keepdims=True)
    for h in range(MLA_HEADS):
        c0 = h * MLA_PAD
        qh = _dot(cqn, wq_ref[:, c0:c0 + MLA_PAD])
        nope = qh[:, :QK_NOPE]
        rope = rotate(qh[:, QK_NOPE:])
        ss = jnp.sum(nope * nope + rope * rope, axis=-1, keepdims=True)
        r = lax.rsqrt(ss * (1.0 / MLA_QK) + EPS)
        q_ref[:, c0:c0 + QK_NOPE] = (nope * r * gq[:, :QK_NOPE]).astype(BF16)
        q_ref[:, c0 + QK_NOPE:c0 + MLA_PAD] = (rope * r * gq[:, QK_NOPE:]).astype(BF16)
        kv = _dot(ckvn, wkv_ref[:, c0:c0 + MLA_PAD])
        kn = kv[:, :QK_NOPE]
        rk = lax.rsqrt((jnp.sum(kn * kn, axis=-1, keepdims=True) + kr_ss) * (1.0 / MLA_QK) + EPS)
        k_ref[:, c0:c0 + QK_NOPE] = (kn * rk * gk[:, :QK_NOPE]).astype(BF16)
        k_ref[:, c0 + QK_NOPE:c0 + MLA_PAD] = (kr * rk * gk[:, QK_NOPE:]).astype(BF16)
        v_ref[:, h * V_HEAD:(h + 1) * V_HEAD] = kv[:, QK_NOPE:].astype(BF16)


def _mla_prep(zc, zr, cs, gcq, gckv, wq, wkv, gq, gk, tm=256):
    T = zc.shape[0]
    row = lambda i: (i, 0)
    fix = lambda i: (0, 0)
    W = MLA_HEADS * MLA_PAD
    return pl.pallas_call(
        _mla_prep_kernel,
        grid=(T // tm,),
        in_specs=[pl.BlockSpec((tm, ZC), row),
                  pl.BlockSpec((tm, LANES), row),
                  pl.BlockSpec((tm, LANES), row),
                  pl.BlockSpec((1, Q_LORA), fix),
                  pl.BlockSpec((1, KV_LORA), fix),
                  pl.BlockSpec((Q_LORA, W), fix),
                  pl.BlockSpec((KV_LORA, W), fix),
                  pl.BlockSpec((1, MLA_PAD), fix),
                  pl.BlockSpec((1, MLA_PAD), fix)],
        out_specs=[pl.BlockSpec((tm, W), row),
                   pl.BlockSpec((tm, W), row),
                   pl.BlockSpec((tm, MLA_OUT), row)],
        out_shape=[jax.ShapeDtypeStruct((T, W), BF16),
                   jax.ShapeDtypeStruct((T, W), BF16),
                   jax.ShapeDtypeStruct((T, MLA_OUT), BF16)],
        compiler_params=_cparams(("parallel",)),
        name="mla_prep",
    )(zc, zr, cs, gcq, gckv, wq, wkv, gq, gk)


DIRECT_SOFTMAX_MAX = 120.0
ATTN_SUB = 256


def _mla_attn_kernel(*refs, tk, nk, online):
    q_ref, k_ref, v_ref = refs[:3]
    o_ref = refs[-1]
    q = q_ref[...]
    tq = q.shape[0]
    sub = tk if online else ATTN_SUB
    ones = jnp.ones((sub, V_HEAD), BF16)

    def chunk(start):
        start = pl.multiple_of(start, sub)
        s = _dot_nt(q, k_ref[pl.ds(start, sub), :])
        vext = jnp.concatenate([v_ref[pl.ds(start, sub), :], ones], axis=1)
        return s, vext

    if online:
        def body(j, carry):
            m, acc = carry
            s, vext = chunk(j * tk)
            m_new = jnp.maximum(m, jnp.max(s, axis=-1, keepdims=True))
            p = jnp.exp2(s - m_new)
            alpha = jnp.exp2(m - m_new)
            return m_new, acc * alpha + _dot(p.astype(BF16), vext)

        init = (jnp.full((tq, 1), NEG, F32), jnp.zeros((tq, 2 * V_HEAD), F32))
        _, acc = lax.fori_loop(0, nk, body, init)
    else:
        def body(j, acc):
            for c in range(tk // sub):
                s, vext = chunk(j * tk + c * sub)
                acc = acc + _dot(jnp.exp2(s).astype(BF16), vext)
            return acc

        acc = lax.fori_loop(0, nk, body, jnp.zeros((tq, 2 * V_HEAD), F32))
    o_ref[...] = (acc[:, :V_HEAD] / acc[:, V_HEAD:]).astype(o_ref.dtype)


def _mla_attn(q, k, v, o_prev, B, S, tok_off, online):
    T = q.shape[0]
    tq, tk = (256, 512) if online else (512, 2048)
    tk = min(tk, S)
    nq = S // tq
    qoff = tok_off // tq
    soff = tok_off // S
    in_specs = [pl.BlockSpec((tq, MLA_PAD), lambda b, h, i: (qoff + b * nq + i, h)),
                pl.BlockSpec((S, MLA_PAD), lambda b, h, i: (soff + b, h)),
                pl.BlockSpec((S, V_HEAD), lambda b, h, i: (soff + b, h))]
    args = [q, k, v]
    aliases = {}
    if o_prev is not None:
        in_specs.append(pl.BlockSpec(memory_space=pl.ANY))
        args.append(o_prev)
        aliases = {3: 0}
    return pl.pallas_call(
        functools.partial(_mla_attn_kernel, tk=tk, nk=S // tk, online=online),
        grid=(B, MLA_HEADS, nq),
        in_specs=in_specs,
        out_specs=pl.BlockSpec((tq, V_HEAD), lambda b, h, i: (qoff + b * nq + i, h)),
        out_shape=jax.ShapeDtypeStruct((T, MLA_OUT), BF16),
        input_output_aliases=aliases,
        compiler_params=_cparams(("parallel", "parallel", "arbitrary")),
        name="mla_attn",
    )(*args)


DIL_TILE = 256
DIL_NOFF = 2 * (DIL_REACH // DIL_TILE) + 1


def _dil_attn_kernel(*refs, nk, online):
    q_ref, k_ref, v_ref, bias_ref = refs[:4]
    i = pl.program_id(2)
    q = q_ref[...]
    t = DIL_TILE
    first = lax.broadcasted_iota(jnp.int32, q.shape, 1) < DIL_HEAD_DIM
    zero = jnp.zeros_like(q)
    half = DIL_NOFF // 2

    if online:
        o_ref, m_s, l_s, acc_s = refs[-4:]
        qh = (jnp.where(first, q, zero), jnp.where(first, zero, q))
        m_s[...] = jnp.full(m_s.shape, NEG, F32)
        l_s[...] = jnp.zeros(l_s.shape, F32)
        acc_s[...] = jnp.zeros(acc_s.shape, F32)
        for d in range(DIL_NOFF):
            j = i + (d - half)

            @pl.when(jnp.logical_and(j >= 0, j < nk))
            def _():
                start = pl.multiple_of(j * t, t)
                kj = k_ref[pl.ds(start, t), :]
                vj = v_ref[pl.ds(start, t), :]
                for hh in range(2):
                    s = _dot_nt(qh[hh], kj) + bias_ref[hh, :, d * t:(d + 1) * t]
                    m_old = m_s[hh]
                    m_new = jnp.maximum(m_old, jnp.max(s, axis=-1, keepdims=True))
                    p = jnp.exp2(s - m_new)
                    alpha = jnp.exp2(m_old - m_new)
                    l_s[hh] = alpha * l_s[hh] + jnp.sum(p, axis=-1, keepdims=True)
                    acc_s[hh] = alpha * acc_s[hh] + _dot(p.astype(BF16), vj)
                    m_s[hh] = m_new

        o0 = acc_s[0] / l_s[0]
        o1 = acc_s[1] / l_s[1]
    else:
        o_ref = refs[-1]
        qq = jnp.concatenate([jnp.where(first, q, zero), jnp.where(first, zero, q)], axis=0)
        ones = jnp.ones((t, LANES), BF16)
        acc = jnp.zeros((2 * t, 2 * LANES), F32)
        for d in range(DIL_NOFF):
            j = i + (d - half)
            valid = jnp.logical_and(j >= 0, j < nk)
            start = pl.multiple_of(jnp.clip(j, 0, nk - 1) * t, t)
            bias = jnp.where(valid, bias_ref[:, :, d * t:(d + 1) * t].reshape(2 * t, t), NEG)
            s = _dot_nt(qq, k_ref[pl.ds(start, t), :]) + bias
            vext = jnp.concatenate([v_ref[pl.ds(start, t), :], ones], axis=1)
            acc = acc + _dot(jnp.exp2(s).astype(BF16), vext)
        o0 = acc[:t, :LANES] / acc[:t, LANES:]
        o1 = acc[t:, :LANES] / acc[t:, LANES:]
    o_ref[...] = jnp.where(first, o0, o1).astype(o_ref.dtype)


def _dil_attn(q, k, v, bias, o_prev, B, S, tok_off, online):
    T = q.shape[0]
    t = DIL_TILE
    nq = S // t
    qoff = tok_off // t
    soff = tok_off // S
    in_specs = [pl.BlockSpec((t, LANES), lambda b, h, i: (qoff + b * nq + i, h)),
                pl.BlockSpec((S, LANES), lambda b, h, i: (soff + b, h)),
                pl.BlockSpec((S, LANES), lambda b, h, i: (soff + b, h)),
                pl.BlockSpec((2, t, DIL_NOFF * t), lambda b, h, i: (h, 0, 0))]
    args = [q, k, v, bias]
    aliases = {}
    if o_prev is not None:
        in_specs.append(pl.BlockSpec(memory_space=pl.ANY))
        args.append(o_prev)
        aliases = {4: 0}
    scratch = [pltpu.VMEM((2, t, 1), F32), pltpu.VMEM((2, t, 1), F32), pltpu.VMEM((2, t, LANES), F32)]
    return pl.pallas_call(
        functools.partial(_dil_attn_kernel, nk=nq, online=online),
        grid=(B, DIL_HEADS // 2, nq),
        in_specs=in_specs,
        out_specs=pl.BlockSpec((t, LANES), lambda b, h, i: (qoff + b * nq + i, h)),
        out_shape=jax.ShapeDtypeStruct((T, DIL_WIDTH), BF16),
        scratch_shapes=scratch if online else [],
        input_output_aliases=aliases,
        compiler_params=_cparams(("parallel", "parallel", "arbitrary")),
        name="dil_attn",
    )(*args)


def _t5_bucket(rel):
    nb = REL_BUCKETS // 2
    max_exact = nb // 2
    sign = (rel > 0).astype(jnp.int32) * nb
    n = jnp.abs(rel)
    large = max_exact + (jnp.log(jnp.maximum(n, 1).astype(F32) / max_exact)
                         / math.log(REL_MAX_DIST / max_exact) * (nb - max_exact)).astype(jnp.int32)
    large = jnp.minimum(large, nb - 1)
    return sign + jnp.where(n < max_exact, n, large)


def _dil_bias_table(rel_bias):
    t = DIL_TILE
    span = DIL_REACH + t - 1
    rel = jnp.arange(-span, span + 1)
    n = jnp.abs(rel)
    count = jnp.zeros(rel.shape, jnp.int32)
    for window, dil in DIL_PATTERNS:
        count = count + ((rel % dil == 0) & (n <= window // 2)).astype(jnp.int32)
    table = rel_bias.astype(F32)[_t5_bucket(rel)].T
    table = (table + jnp.log(jnp.maximum(count, 1).astype(F32))[None]) * LOG2E
    table = jnp.where((count > 0)[None], table, NEG)
    width = rel.shape[0]
    padded = jnp.pad(table, ((0, 0), (0, 1)))
    shifts = jnp.tile(padded, (1, t))[:, :t * width].reshape(DIL_HEADS, t, width)
    return shifts[:, :, t - 1:t - 1 + DIL_NOFF * t]


def _out_proj_kernel(x_ref, om_ref, od_ref, wa_ref, wb_ref, g_ref, x1_ref, h2_ref):
    x1 = x_ref[...] + _dot(om_ref[...], wa_ref[...]) + _dot(od_ref[...], wb_ref[...])
    x1_ref[...] = x1
    h2_ref[...] = _rms(x1, g_ref[...]).astype(BF16)


def _out_proj(x, o_mla, o_dil, wa, wb, g_ffn, tm=256):
    T = x.shape[0]
    row = lambda i: (i, 0)
    fix = lambda i: (0, 0)
    return pl.pallas_call(
        _out_proj_kernel,
        grid=(T // tm,),
        in_specs=[pl.BlockSpec((tm, D_MODEL), row),
                  pl.BlockSpec((tm, MLA_OUT), row),
                  pl.BlockSpec((tm, DIL_WIDTH), row),
                  pl.BlockSpec((MLA_OUT, D_MODEL), fix),
                  pl.BlockSpec((DIL_WIDTH, D_MODEL), fix),
                  pl.BlockSpec((1, D_MODEL), fix)],
        out_specs=[pl.BlockSpec((tm, D_MODEL), row),
                   pl.BlockSpec((tm, D_MODEL), row)],
        out_shape=[jax.ShapeDtypeStruct((T, D_MODEL), F32),
                   jax.ShapeDtypeStruct((T, D_MODEL), BF16)],
        compiler_params=_cparams(("parallel",)),
        name="out_proj",
    )(x, o_mla, o_dil, wa, wb, g_ffn)


PEER_HC = PEER_HEADS * 2
PEER_HALF = PEER_DKEY // 2


def _peer_query_kernel(h2_ref, w_ref, keys_ref, s_ref):
    pq = _dot(h2_ref[...], w_ref[...]).astype(BF16)
    for hc in range(PEER_HC):
        s_ref[hc] = _dot_nt(keys_ref[hc], pq[:, hc * PEER_HALF:(hc + 1) * PEER_HALF])


def _peer_query(h2, w_pq, keys, tm=256):
    T = h2.shape[0]
    return pl.pallas_call(
        _peer_query_kernel,
        grid=(T // tm,),
        in_specs=[pl.BlockSpec((tm, D_MODEL), lambda i: (i, 0)),
                  pl.BlockSpec((D_MODEL, PEER_HEADS * PEER_DKEY), lambda i: (0, 0)),
                  pl.BlockSpec((PEER_HC, PEER_NKEYS, PEER_HALF), lambda i: (0, 0, 0))],
        out_specs=pl.BlockSpec((PEER_HC, PEER_NKEYS, tm), lambda i: (0, 0, i)),
        out_shape=jax.ShapeDtypeStruct((PEER_HC, PEER_NKEYS, T), F32),
        compiler_params=_cparams(("parallel",)),
        name="peer_query",
    )(h2, w_pq, keys)


def _next_max(x, prev):
    return jnp.max(jnp.where(x < prev, x, -jnp.inf), axis=0, keepdims=True)


def _peer_topk_kernel(s_ref, thr_ref, e1_ref, e2_ref, top_s):
    K = PEER_TOPK
    inf_row = jnp.full((1, s_ref.shape[-1]), jnp.inf, F32)

    def top_body(k, prevs):
        mxs = [_next_max(s_ref[hc // 2, hc % 2], prevs[hc]) for hc in range(PEER_HC)]
        for hc, mx in enumerate(mxs):
            top_s[hc, pl.ds(k, 1), :] = mx
        return mxs

    lax.fori_loop(0, K, top_body, [inf_row] * PEER_HC)

    cands = []
    for h in range(PEER_HEADS):
        v1 = top_s[2 * h]
        v2 = top_s[2 * h + 1]
        rows = [v1 + v2[0:1]]
        rows += [v1[:SUBLANES] + v2[k2:k2 + 1] for k2 in range(1, SUBLANES)]
        rows += [v1[0:1] + v2[SUBLANES:]]
        cands.append(jnp.concatenate(rows, axis=0))

    def tau_body(k, prevs):
        return [_next_max(c, p) for c, p in zip(cands, prevs)]

    taus = lax.fori_loop(0, K, tau_body, [inf_row] * PEER_HEADS)

    for h in range(PEER_HEADS):
        a = s_ref[h, 0]
        b = s_ref[h, 1]
        amax = top_s[2 * h, 0:1]
        bmax = top_s[2 * h + 1, 0:1]
        cand = cands[h]
        z = jnp.sum(jnp.where(cand >= taus[h], jnp.exp(cand - (amax + bmax)), 0.0), axis=0, keepdims=True)
        thr_ref[h] = taus[h] - a
        e1_ref[h] = jnp.exp(a - amax)
        e2_ref[h] = jnp.exp(b - bmax) / z


def _peer_topk(s4, tl=128):
    T = s4.shape[-1]
    out = jax.ShapeDtypeStruct((PEER_HEADS, PEER_NKEYS, T), F32)
    ospec = pl.BlockSpec((PEER_HEADS, PEER_NKEYS, tl), lambda i: (0, 0, i))
    return pl.pallas_call(
        _peer_topk_kernel,
        grid=(T // tl,),
        in_specs=[pl.BlockSpec((PEER_HEADS, 2, PEER_NKEYS, tl), lambda i: (0, 0, 0, i))],
        out_specs=[ospec, ospec, ospec],
        out_shape=[out, out, out],
        scratch_shapes=[pltpu.VMEM((PEER_HC, PEER_TOPK, tl), F32)],
        compiler_params=_cparams(("parallel",)),
        name="peer_topk",
    )(s4)


PEER_TB = 512
PEER_NE1 = 8
PEER_CHAIN = 2


def _peer_gate_kernel(thr_ref, e1_ref, b_ref, e2_ref, g_ref):
    j = pl.program_id(1)
    n = PEER_NKEYS
    for l in range(PEER_NE1):
        first = j * PEER_NE1 + l
        g = jnp.zeros((n, g_ref.shape[1]), F32)
        for h in range(PEER_HEADS):
            thr = thr_ref[h, pl.ds(first, 1), :]
            w1 = e1_ref[h, pl.ds(first, 1), :]
            g = g + jnp.where(b_ref[h] >= thr, e2_ref[h] * w1, 0.0)
        g_ref[l * n:(l + 1) * n, :] = g.astype(BF16)


def _peer_gate(thr, e1, s4, e2):
    T = thr.shape[-1]
    tb = PEER_TB
    ec = PEER_NE1 * PEER_NKEYS
    tok = pl.BlockSpec((PEER_HEADS, PEER_NKEYS, tb), lambda i, j: (0, 0, i))
    return pl.pallas_call(
        _peer_gate_kernel,
        grid=(T // tb, PEER_NKEYS // PEER_NE1),
        in_specs=[tok, tok,
                  pl.BlockSpec((PEER_HEADS, None, PEER_NKEYS, tb), lambda i, j: (0, 1, 0, i)),
                  tok],
        out_specs=pl.BlockSpec((ec, tb), lambda i, j: (j, i)),
        out_shape=jax.ShapeDtypeStruct((PEER_NKEYS * PEER_NKEYS, T), BF16),
        compiler_params=_cparams(("parallel", "arbitrary")),
        name="peer_gate",
    )(thr, e1, s4, e2)


def _peer_expert_kernel(h2_ref, u_ref, v_ref, g_ref, x1_ref, y_ref):
    j = pl.program_id(1)

    @pl.when(j == 0)
    def _():
        y_ref[...] = x1_ref[...]

    h2 = h2_ref[...]
    rows = PEER_CHAIN * PEER_NKEYS
    acc = None
    for c in range(PEER_NE1 // PEER_CHAIN):
        at = _dot_nt(u_ref[c * rows:(c + 1) * rows, :], h2)
        gate = g_ref[c * rows:(c + 1) * rows, :].astype(F32)
        coef = (gate * jax.nn.gelu(at)).astype(BF16)
        part = _dot_tn(coef, v_ref[c * rows:(c + 1) * rows, :])
        acc = part if acc is None else acc + part
    y_ref[...] += acc


def _peer_expert(h2, u, v, gates, x1):
    T = h2.shape[0]
    tb = PEER_TB
    ec = PEER_NE1 * PEER_NKEYS
    once = pl.Buffered(1)
    return pl.pallas_call(
        _peer_expert_kernel,
        grid=(T // tb, u.shape[0] // ec),
        in_specs=[pl.BlockSpec((tb, D_MODEL), lambda i, j: (i, 0), pipeline_mode=once),
                  pl.BlockSpec((ec, D_MODEL), lambda i, j: (j, 0)),
                  pl.BlockSpec((ec, D_MODEL), lambda i, j: (j, 0)),
                  pl.BlockSpec((ec, tb), lambda i, j: (j, i)),
                  pl.BlockSpec((tb, D_MODEL), lambda i, j: (i, 0), pipeline_mode=once)],
        out_specs=pl.BlockSpec((tb, D_MODEL), lambda i, j: (i, 0)),
        out_shape=jax.ShapeDtypeStruct((T, D_MODEL), F32),
        compiler_params=_cparams(("parallel", "arbitrary")),
        name="peer_expert",
    )(h2, u, v, gates, x1)


def _rope_table(segments):
    half = QK_ROPE // 2
    inv = ROPE_THETA ** (-jnp.arange(half, dtype=F32) / half)
    pos = jnp.concatenate([jnp.tile(jnp.arange(S), B) for B, S, _ in segments])
    ang = pos.astype(F32)[:, None] * inv[None, :]
    cos, sin = jnp.cos(ang), jnp.sin(ang)
    return jnp.concatenate([cos, cos, sin, sin], axis=-1)


def _rot_half_cols(w):
    half = QK_ROPE // 2
    return jnp.concatenate([-w[..., half:], w[..., :half]], axis=-1)


def _col_norm_max(w):
    return jnp.sqrt(jnp.max(jnp.sum(jnp.square(w.astype(F32)), axis=0)))


def _direct_softmax_ok(score_bound, n_keys, value_bound):
    return score_bound + jnp.log2(n_keys * jnp.maximum(value_bound, 1.0)) <= DIRECT_SOFTMAX_MAX


def kernel(x_prompt, x_sample, g_attn, w_in, g_cq, w_uq, g_ckv, w_ukv, g_mla_qn, g_mla_kn,
           g_dil_qn, g_dil_kn, rel_bias, w_out, g_ffn, w_pq, peer_subkeys, peer_u, peer_v):
    segments = []
    off = 0
    for xs in (x_prompt, x_sample):
        B, S, _ = xs.shape
        segments.append((B, S, off))
        off += B * S
    T = off
    x = jnp.concatenate([x_prompt.reshape(-1, D_MODEL), x_sample.reshape(-1, D_MODEL)], axis=0)
    l = 0

    o = [0, Q_LORA, Q_LORA + KV_LORA, Q_LORA + KV_LORA + QK_ROPE]
    wi = w_in[l]
    w_cq, w_ckv, w_kr, w_dil = wi[:, o[0]:o[1]], wi[:, o[1]:o[2]], wi[:, o[2]:o[3]], wi[:, o[3]:]
    w1 = jnp.concatenate([w_cq, w_ckv, w_dil, w_kr, _rot_half_cols(w_kr)], axis=1).astype(BF16)
    head_of = jnp.arange(DIL_WIDTH) // DIL_HEAD_DIM
    seg = (head_of[:, None] == jnp.arange(LANES)[None, :]).astype(BF16)
    segt = seg.T
    dil_scale = DIL_HEAD_DIM ** -0.5 * LOG2E
    gq_d = (jnp.tile(g_dil_qn[l], DIL_HEADS) * dil_scale)[None]
    gk_d = jnp.tile(g_dil_kn[l], DIL_HEADS)[None]

    wq3 = w_uq[l].reshape(Q_LORA, MLA_HEADS, MLA_QK)
    wq = jnp.concatenate([wq3, _rot_half_cols(wq3[..., QK_NOPE:])], axis=-1)
    wq = wq.reshape(Q_LORA, MLA_HEADS * MLA_PAD).astype(BF16)
    wkv = w_ukv[l].astype(BF16)
    pad = jnp.zeros((MLA_PAD - MLA_QK,), F32)
    mla_scale = MLA_QK ** -0.5 * LOG2E
    gq_m = jnp.concatenate([g_mla_qn[l] * mla_scale, pad])[None]
    gk_m = jnp.concatenate([g_mla_kn[l], pad])[None]
    cs = _rope_table(segments)
    bias = _dil_bias_table(rel_bias)

    wo = w_out[l].astype(BF16)
    keys = peer_subkeys[l].reshape(PEER_HC, PEER_NKEYS, PEER_HALF).astype(BF16)
    u = peer_u[l].astype(BF16)
    v = peer_v[l].astype(BF16)

    amax = lambda a: jnp.max(jnp.abs(a))
    s_max = max(S for _, S, _ in segments)
    mla_ok = _direct_softmax_ok(
        MLA_QK * amax(g_mla_qn[l]) * amax(g_mla_kn[l]) * mla_scale, s_max,
        KV_LORA ** 0.5 * amax(g_ckv[l]) * _col_norm_max(w_ukv[l]))
    dil_ok = _direct_softmax_ok(
        DIL_HEAD_DIM * amax(g_dil_qn[l]) * amax(g_dil_kn[l]) * dil_scale + amax(rel_bias) * LOG2E + 2.0,
        DIL_NOFF * DIL_TILE,
        D_MODEL ** 0.5 * amax(g_attn[l]) * _col_norm_max(w_dil[:, 2 * DIL_WIDTH:]))

    zc, zr, qd, kd, vd = _in_proj(x, g_attn[l][None], w1, seg, segt, gq_d, gk_d)
    q, k, vv = _mla_prep(zc, zr, cs, g_cq[l][None], g_ckv[l][None], wq, wkv, gq_m, gk_m)

    def attend(online):
        o_mla = None
        o_dil = None
        for B, S, toff in segments:
            o_mla = _mla_attn(q, k, vv, o_mla, B, S, toff, online)
            o_dil = _dil_attn(qd, kd, vd, bias, o_dil, B, S, toff, online)
        return o_mla, o_dil

    o_mla, o_dil = lax.cond(jnp.logical_and(mla_ok, dil_ok), lambda: attend(False), lambda: attend(True))
    x1, h2 = _out_proj(x, o_mla, o_dil, wo[:MLA_OUT], wo[MLA_OUT:], g_ffn[l][None])
    s = _peer_query(h2, w_pq[l].astype(BF16), keys)
    s4 = s.reshape(PEER_HEADS, 2, PEER_NKEYS, T)
    thr, e1, e2 = _peer_topk(s4)
    y = _peer_expert(h2, u, v, _peer_gate(thr, e1, s4, e2), x1)
    n_prompt = x_prompt.shape[0] * x_prompt.shape[1]
    return (y[:n_prompt].reshape(x_prompt.shape), y[n_prompt:].reshape(x_sample.shape))
```
